```python
import math
import jax, jax.numpy as jnp
from jax import lax
import numpy as np

D_MODEL = 1024
BATCH = 2
SEQ = 16384
DEPTH = 2

N_MIXERS = 2
EXPAND = 2
D_INNER = EXPAND * D_MODEL
EPS = 1e-6

HG_DK = 128
HG_HEADS = D_INNER // HG_DK
HG_DV = D_INNER // HG_HEADS
HG_CHUNK = 32

ATT_HEAD_DIM = 128
ATT_HEADS = D_INNER // ATT_HEAD_DIM
DILATED_GROUPS = ((128, 1), (512, 4), (2048, 16))
N_GROUPS = len(DILATED_GROUPS)
ATT_BLOCK = 128
N_BUCKETS = 32
MAX_DISTANCE = 2048

N_A = (DEPTH + 1) // 2
N_B = DEPTH // 2
ATT_IN_COLS = (3 * N_GROUPS + 1) * D_INNER
HG_IN_COLS = 4 * D_INNER

kernel_name = "hgrn2_dilated_swa_interleaved_hybrid"


def rmsnorm(x, g):
    xf = x.astype(jnp.float32)
    y = xf * lax.rsqrt(jnp.mean(xf * xf, axis=-1, keepdims=True) + EPS)
    return (y * g.astype(jnp.float32)).astype(x.dtype)


def proj_cols(u, w, start, width):
    return u @ w[:, start:start + width]


def hgrn2_mixer(u, w_in, lb, norm_g, w_out):
    B, S, _ = u.shape
    W = D_INNER
    q = proj_cols(u, w_in, 0, W)
    f_pre = proj_cols(u, w_in, W, W)
    i_in = proj_cols(u, w_in, 2 * W, W)
    gate = proj_cols(u, w_in, 3 * W, W)

    f = lb + (1.0 - lb) * jax.nn.sigmoid(f_pre.astype(jnp.float32))
    log_f = jnp.log(f)
    k = 1.0 - f
    n_chunks = S // HG_CHUNK

    def chunks(t):
        t = t.astype(jnp.float32).reshape(B, n_chunks, HG_CHUNK, HG_HEADS, -1)
        return t.transpose(1, 0, 3, 2, 4)

    causal = jnp.tril(jnp.ones((HG_CHUNK, HG_CHUNK), dtype=bool))

    def step(state, xs):
        q_c, k_c, lf_c, v_c = xs
        b = jnp.cumsum(lf_c, axis=2)
        b_last = b[:, :, -1:, :]
        q_dec = q_c * jnp.exp(b)
        scores = jnp.einsum('bhcd,bhsd->bhcs', q_dec, k_c * jnp.exp(-b))
        scores = jnp.where(causal, scores, 0.0)
        o = (jnp.einsum('bhcs,bhse->bhce', scores, v_c)
             + jnp.einsum('bhcd,bhde->bhce', q_dec, state))
        state = (jnp.exp(b_last[:, :, 0, :])[..., None] * state
                 + jnp.einsum('bhcd,bhce->bhde', k_c * jnp.exp(b_last - b), v_c))
        return state, o

    s0 = jnp.zeros((B, HG_HEADS, HG_DK, HG_DV), jnp.float32)
    _, o = lax.scan(step, s0, (chunks(q), chunks(k), chunks(log_f), chunks(i_in)))
    o = o.transpose(1, 0, 3, 2, 4).reshape(B, S, HG_HEADS, HG_DV)
    o = o * lax.rsqrt(jnp.mean(o * o, axis=-1, keepdims=True) + EPS)
    o = o.reshape(B, S, D_INNER) * norm_g.astype(jnp.float32)
    y = o * jax.nn.silu(gate.astype(jnp.float32))
    return y.astype(u.dtype) @ w_out


def t5_bucket(dist):
    max_exact = N_BUCKETS // 2
    df = jnp.maximum(dist, 1).astype(jnp.float32)
    large = max_exact + (jnp.log(df / max_exact) / math.log(MAX_DISTANCE / max_exact)
                         * (N_BUCKETS - max_exact)).astype(jnp.int32)
    large = jnp.minimum(large, N_BUCKETS - 1)
    return jnp.where(dist < max_exact, dist, large)


def dilated_group(q, k, v, bias_table, window, dilation):
    B, S, H, E = q.shape
    d = dilation
    L = S // d
    n_blk = -(-L // ATT_BLOCK)
    Lp = n_blk * ATT_BLOCK
    sub_window = window // d

    def by_residue(t):
        t = t.reshape(B, L, d, H, E).transpose(0, 2, 1, 3, 4)
        return jnp.pad(t, ((0, 0), (0, 0), (0, Lp - L), (0, 0), (0, 0)))

    def banded(t):
        t = jnp.pad(t, ((0, 0), (0, 0), (ATT_BLOCK, 0), (0, 0), (0, 0)))
        t = t.reshape(B, d, n_blk + 1, ATT_BLOCK, H, E)
        return jnp.concatenate([t[:, :, :-1], t[:, :, 1:]], axis=3)

    qb = by_residue(q).reshape(B, d, n_blk, ATT_BLOCK, H, E)
    kb = banded(by_residue(k))
    vb = banded(by_residue(v))

    a = jnp.arange(ATT_BLOCK)[:, None]
    c = jnp.arange(2 * ATT_BLOCK)[None, :]
    rel = ATT_BLOCK + a - c
    band = (rel >= 0) & (rel <= sub_window)
    kpos = (jnp.arange(n_blk)[:, None, None] - 1) * ATT_BLOCK + c[None]
    mask = band[None] & (kpos >= 0)
    bucket = t5_bucket(jnp.maximum(rel, 0) * d)
    bias = bias_table[bucket].astype(jnp.float32).transpose(2, 0, 1)

    s = jnp.einsum('brnqhe,brnkhe->brnhqk', qb, kb,
                   preferred_element_type=jnp.float32) * (E ** -0.5) + bias
    s = jnp.where(mask[None, None, :, None], s, -1e30)
    m = jnp.max(s, axis=-1, keepdims=True)
    p = jnp.exp(s - m)
    den = jnp.sum(p, axis=-1)
    o = jnp.einsum('brnhqk,brnkhe->brnqhe', p, vb.astype(jnp.float32))
    o = o / den.transpose(0, 1, 2, 4, 3)[..., None]
    lse = (m[..., 0] + jnp.log(den)).transpose(0, 1, 2, 4, 3)

    o = o.reshape(B, d, Lp, H, E)[:, :, :L].transpose(0, 2, 1, 3, 4).reshape(B, S, H, E)
    lse = lse.reshape(B, d, Lp, H)[:, :, :L].transpose(0, 2, 1, 3).reshape(B, S, H)
    return o, lse


def dilated_attention_mixer(u, w_in, rel_bias, w_out):
    B, S, _ = u.shape
    W = D_INNER
    outs, lses = [], []
    for g, (window, dilation) in enumerate(DILATED_GROUPS):
        base = 3 * g * W
        q = proj_cols(u, w_in, base, W).reshape(B, S, ATT_HEADS, ATT_HEAD_DIM)
        k = proj_cols(u, w_in, base + W, W).reshape(B, S, ATT_HEADS, ATT_HEAD_DIM)
        v = proj_cols(u, w_in, base + 2 * W, W).reshape(B, S, ATT_HEADS, ATT_HEAD_DIM)
        table = rel_bias[:, g * ATT_HEADS:(g + 1) * ATT_HEADS]
        o, lse = dilated_group(q, k, v, table, window, dilation)
        outs.append(o)
        lses.append(lse)
    wts = jax.nn.softmax(jnp.stack(lses, axis=0), axis=0)
    o = jnp.einsum('gbsh,gbshe->bshe', wts, jnp.stack(outs, axis=0)).reshape(B, S, W)
    gate = proj_cols(u, w_in, 3 * N_GROUPS * W, W)
    y = o * jax.nn.silu(gate.astype(jnp.float32))
    return y.astype(u.dtype) @ w_out


def setup_inputs(seed: int = 0) -> dict:
    key = jax.random.key(seed)
    ks = jax.random.split(key, 10)
    f32 = jnp.float32
    x = jax.random.normal(ks[0], (BATCH, SEQ, D_MODEL), f32)
    ln_g = 1.0 + 0.02 * jax.random.normal(ks[1], (DEPTH, D_MODEL), f32)
    hg_w_in = jax.random.normal(ks[2], (N_A, D_MODEL, HG_IN_COLS), f32) * D_MODEL ** -0.5
    hg_lb_logits = 0.1 * jax.random.normal(ks[3], (DEPTH + 1, D_INNER), f32)
    hg_norm_g = 1.0 + 0.02 * jax.random.normal(ks[4], (N_A, D_INNER), f32)
    hg_w_out = jax.random.normal(ks[5], (N_A, D_INNER, D_MODEL), f32) * D_INNER ** -0.5
    att_w_in = jax.random.normal(ks[6], (N_B, D_MODEL, ATT_IN_COLS), f32) * D_MODEL ** -0.5
    att_w_out = jax.random.normal(ks[7], (N_B, D_INNER, D_MODEL), f32) * D_INNER ** -0.5
    rel_bias = 0.2 * jax.random.normal(ks[8], (N_BUCKETS, N_GROUPS * ATT_HEADS), f32)
    final_g = 1.0 + 0.02 * jax.random.normal(ks[9], (D_MODEL,), f32)
    return {"x": x, "ln_g": ln_g, "hg_w_in": hg_w_in, "hg_lb_logits": hg_lb_logits,
            "hg_norm_g": hg_norm_g, "hg_w_out": hg_w_out, "att_w_in": att_w_in,
            "att_w_out": att_w_out, "rel_bias": rel_bias, "final_g": final_g}


def reference(x, ln_g, hg_w_in, hg_lb_logits, hg_norm_g, hg_w_out, att_w_in, att_w_out,
              rel_bias, final_g):
    lower_bounds = jnp.cumsum(jax.nn.softmax(hg_lb_logits.astype(jnp.float32), axis=0), axis=0)
    h = x
    for i in range(DEPTH):
        u = rmsnorm(h, ln_g[i])
        j = i // N_MIXERS
        if i % N_MIXERS == 0:
            y = hgrn2_mixer(u, hg_w_in[j], lower_bounds[i], hg_norm_g[j], hg_w_out[j])
        else:
            y = dilated_attention_mixer(u, att_w_in[j], rel_bias, att_w_out[j])
        h = h + y.astype(h.dtype)
    return rmsnorm(h, final_g)
```

```python
import functools
import math

import jax
import jax.numpy as jnp
from jax import lax
from jax.experimental import pallas as pl
from jax.experimental.pallas import tpu as pltpu

EPS = 1e-6
HEAD = 128
HG_CHUNK = 64
ATT_BLOCK = 128
DILATED_GROUPS = ((128, 1), (512, 4), (2048, 16))
N_GROUPS = len(DILATED_GROUPS)
N_BUCKETS = 32
MAX_DISTANCE = 2048
ATT_TILE = ATT_BLOCK * max(d for _, d in DILATED_GROUPS)
NEG = -1e30

VMEM_LIMIT = 56 * 1024 * 1024

f32 = jnp.float32
bf16 = jnp.bfloat16


def _cparams(sem):
    return pltpu.CompilerParams(dimension_semantics=sem, vmem_limit_bytes=VMEM_LIMIT)


def _rmsnorm_kernel(x_ref, g_ref, o_ref):
    x = x_ref[...]
    ms = jnp.mean(x * x, axis=-1, keepdims=True)
    o_ref[...] = (x * lax.rsqrt(ms + EPS) * g_ref[...]).astype(o_ref.dtype)


def _rmsnorm(x, g, out_dtype, tm=1024):
    n, d = x.shape
    return pl.pallas_call(
        _rmsnorm_kernel,
        out_shape=jax.ShapeDtypeStruct((n, d), out_dtype),
        grid=(n // tm,),
        in_specs=[pl.BlockSpec((tm, d), lambda i: (i, 0)),
                  pl.BlockSpec((1, d), lambda i: (0, 0))],
        out_specs=pl.BlockSpec((tm, d), lambda i: (i, 0)),
        compiler_params=_cparams(("parallel",)),
        name="rmsnorm",
    )(x, g.reshape(1, d))


def _matmul_kernel(a_ref, w_ref, o_ref, *, row_chunk):
    tm = a_ref.shape[0]
    for r in range(tm // row_chunk):
        rows = slice(r * row_chunk, (r + 1) * row_chunk)
        o_ref[rows, :] = jnp.dot(a_ref[rows, :], w_ref[...], preferred_element_type=f32).astype(o_ref.dtype)


def _matmul(a, w, out_dtype, tm, tn):
    n, k = a.shape
    c = w.shape[1]
    tm = min(tm, n)
    tn = math.gcd(tn, c)
    return pl.pallas_call(
        functools.partial(_matmul_kernel, row_chunk=min(512, tm)),
        out_shape=jax.ShapeDtypeStruct((n, c), out_dtype),
        grid=(n // tm, c // tn),
        in_specs=[pl.BlockSpec((tm, k), lambda i, j: (i, 0)),
                  pl.BlockSpec((k, tn), lambda i, j: (0, j))],
        out_specs=pl.BlockSpec((tm, tn), lambda i, j: (i, j)),
        compiler_params=_cparams(("parallel", "arbitrary")),
        name="proj_matmul",
    )(a, w)


def _outproj_kernel(y_ref, w_ref, h_ref, g_ref, *out_refs, emit_h):
    h = h_ref[...] + jnp.dot(y_ref[...], w_ref[...], preferred_element_type=f32)
    if emit_h:
        out_refs[0][...] = h
    n_ref = out_refs[-1]
    ms = jnp.mean(h * h, axis=-1, keepdims=True)
    n_ref[...] = (h * lax.rsqrt(ms + EPS) * g_ref[...]).astype(n_ref.dtype)


def _outproj(y, w, h, g, norm_dtype, emit_h, tm=512):
    n, k = y.shape
    d = w.shape[1]
    tm = min(tm, n)
    row = lambda i: (i, 0)
    out_shape = [jax.ShapeDtypeStruct((n, d), norm_dtype)]
    out_specs = [pl.BlockSpec((tm, d), row)]
    if emit_h:
        out_shape.insert(0, jax.ShapeDtypeStruct((n, d), f32))
        out_specs.insert(0, pl.BlockSpec((tm, d), row))
    return pl.pallas_call(
        functools.partial(_outproj_kernel, emit_h=emit_h),
        out_shape=out_shape,
        grid=(n // tm,),
        in_specs=[pl.BlockSpec((tm, k), row),
                  pl.BlockSpec((k, d), lambda i: (0, 0)),
                  pl.BlockSpec((tm, d), row),
                  pl.BlockSpec((1, d), lambda i: (0, 0))],
        out_specs=out_specs,
        compiler_params=_cparams(("parallel",)),
        name="outproj",
    )(y, w, h, g.reshape(1, d))


def _sigmoid(x):
    return 1.0 / (1.0 + jnp.exp(-x))


def _hgrn_kernel(q_ref, f_ref, i_ref, g_ref, lb_ref, ng_ref, o_ref, st_ref, *, n_chunks):
    c = HG_CHUNK
    half = c // 2

    @pl.when(pl.program_id(2) == 0)
    def _():
        st_ref[...] = jnp.zeros_like(st_ref)

    lb = lb_ref[...]
    ng = ng_ref[...]
    row = lax.broadcasted_iota(jnp.int32, (c, c), 0)
    col = lax.broadcasted_iota(jnp.int32, (c, c), 1)
    causal = row >= col
    tril = causal.astype(f32)

    def chunk(ci, carry):
        rows = pl.ds(pl.multiple_of(ci * c, c), c)
        f = lb + (1.0 - lb) * _sigmoid(f_ref[rows, :])
        lf = jnp.log(f)
        k = 1.0 - f
        b = jnp.dot(tril, lf, precision=lax.Precision.HIGHEST, preferred_element_type=f32)
        b_mid = b[half - 1:half, :]
        b_last = b[c - 1:c, :]
        qm = q_ref[rows, :] * jnp.exp(b - b_mid)
        kd = (k * jnp.exp(b_mid - b)).astype(bf16)
        q0 = (qm * jnp.exp(b_mid)).astype(bf16)
        qd = qm.astype(bf16)
        v = i_ref[rows, :].astype(bf16)
        decay = jnp.exp(b_last)
        u_scale = jnp.exp(b_last - b_mid)
        gate = g_ref[rows, :]
        for h in range(2):
            ln = slice(h * HEAD, (h + 1) * HEAD)
            scores = lax.dot_general(qd[:, ln], kd[:, ln], (((1,), (1,)), ((), ())),
                                     preferred_element_type=f32)
            scores = jnp.where(causal, scores, 0.0).astype(bf16)
            st = st_ref[h]
            o = (jnp.dot(scores, v[:, ln], preferred_element_type=f32)
                 + lax.dot_general(q0[:, ln], st.astype(bf16), (((1,), (1,)), ((), ())),
                                   preferred_element_type=f32))
            upd = lax.dot_general(v[:, ln], kd[:, ln], (((0,), (0,)), ((), ())),
                                  preferred_element_type=f32)
            st_ref[h] = st * decay[:, ln] + upd * u_scale[:, ln]
            o = o * lax.rsqrt(jnp.mean(o * o, axis=-1, keepdims=True) + EPS)
            gh = gate[:, ln]
            o_ref[rows, ln] = (o * ng[:, ln] * (gh * _sigmoid(gh))).astype(o_ref.dtype)
        return carry

    lax.fori_loop(0, n_chunks, chunk, 0)


def _hgrn(p, lb, ng, batch, seq, tt):
    n, c4 = p.shape
    w = c4 // 4
    pairs = w // (2 * HEAD)
    tt = min(tt, seq)
    nt = seq // tt
    blk = (tt, 2 * HEAD)

    def spec(k):
        return pl.BlockSpec(blk, lambda b, pr, t, k=k: (b * nt + t, k * pairs + pr))

    vec = pl.BlockSpec((1, 2 * HEAD), lambda b, pr, t: (0, pr))
    return pl.pallas_call(
        functools.partial(_hgrn_kernel, n_chunks=tt // HG_CHUNK),
        out_shape=jax.ShapeDtypeStruct((n, w), bf16),
        grid=(batch, pairs, nt),
        in_specs=[spec(0), spec(1), spec(2), spec(3), vec, vec],
        out_specs=pl.BlockSpec(blk, lambda b, pr, t: (b * nt + t, pr)),
        scratch_shapes=[pltpu.VMEM((2, HEAD, HEAD), f32)],
        compiler_params=_cparams(("parallel", "parallel", "arbitrary")),
        name="hgrn2",
    )(p, p, p, p, lb.reshape(1, w), ng.reshape(1, w))


def _kv_layout(d):
    per_res = ATT_TILE // d
    return per_res, per_res + ATT_BLOCK


def _attn_kernel(q0_ref, k0_ref, v0_ref, gate_ref, q1_ref, k1_ref, v1_ref, q2_ref, k2_ref, v2_ref,
                 bias_ref, o_ref, kb0, vb0, kb1, vb1, kb2, vb2, oacc, lacc):
    t = pl.program_id(2)
    q_refs = (q0_ref, q1_ref, q2_ref)
    k_refs = (k0_ref, k1_ref, k2_ref)
    v_refs = (v0_ref, v1_ref, v2_ref)
    kbufs = (kb0, kb1, kb2)
    vbufs = (vb0, vb1, vb2)
    scale = HEAD ** -0.5
    blk = ATT_BLOCK

    @pl.when(t == 0)
    def _():
        for g, (_, d) in enumerate(DILATED_GROUPS):
            _, stride = _kv_layout(d)
            for r in range(d):
                kbufs[g][r * stride:r * stride + blk, :] = jnp.zeros((blk, HEAD), bf16)
                vbufs[g][r * stride:r * stride + blk, :] = jnp.zeros((blk, HEAD), bf16)

    colid = lax.broadcasted_iota(jnp.int32, (blk, 2 * blk), 1)
    first = jnp.where(t == 0, NEG, 0.0).astype(f32)
    pen = jnp.where(colid < blk, first, 0.0)

    for g, (_, d) in enumerate(DILATED_GROUPS):
        per_res, stride = _kv_layout(d)
        nblk = per_res // blk
        kb, vb = kbufs[g], vbufs[g]
        for r in range(d):
            kb[r * stride + blk:(r + 1) * stride, :] = k_refs[g][r * per_res:(r + 1) * per_res, :]
            vb[r * stride + blk:(r + 1) * stride, :] = v_refs[g][r * per_res:(r + 1) * per_res, :]
        bias = bias_ref[g]
        bias_first = bias + pen

        def block(idx, carry, g=g, d=d, per_res=per_res, stride=stride, nblk=nblk, kb=kb, vb=vb,
                  bias=bias, bias_first=bias_first):
            r = idx // nblk
            j = idx % nblk
            q = q_refs[g][pl.ds(pl.multiple_of(r * per_res + j * blk, blk), blk), :]
            win = pl.ds(pl.multiple_of(r * stride + j * blk, blk), 2 * blk)
            s = lax.dot_general(q, kb[win, :], (((1,), (1,)), ((), ())), preferred_element_type=f32)
            s = s * scale + jnp.where(j == 0, bias_first, bias)
            m = jnp.max(s, axis=-1, keepdims=True)
            p = jnp.exp(s - m)
            l = jnp.sum(p, axis=-1, keepdims=True)
            o = jnp.dot(p.astype(bf16), vb[win, :], preferred_element_type=f32)
            o = o / l
            lse = jnp.broadcast_to(m + jnp.log(l), (blk, HEAD))
            if d == 1:
                dst = pl.ds(pl.multiple_of(j * blk, blk), blk)
            else:
                dst = pl.ds(j * blk * d + r, blk, stride=d)
            oacc[g, dst, :] = o
            lacc[g, dst, :] = lse
            return carry

        lax.fori_loop(0, d * nblk, block, 0)

        for r in range(d):
            kb[r * stride:r * stride + blk, :] = kb[(r + 1) * stride - blk:(r + 1) * stride, :]
            vb[r * stride:r * stride + blk, :] = vb[(r + 1) * stride - blk:(r + 1) * stride, :]

    rc = 256
    for c0 in range(0, ATT_TILE, rc):
        rows = slice(c0, c0 + rc)
        ls = [lacc[g, rows, :] for g in range(N_GROUPS)]
        mx = functools.reduce(jnp.maximum, ls)
        ws = [jnp.exp(x - mx) for x in ls]
        num = functools.reduce(lambda a, b: a + b, [w * oacc[g, rows, :] for g, w in enumerate(ws)])
        den = functools.reduce(lambda a, b: a + b, ws)
        gate = gate_ref[rows, :].astype(f32)
        o_ref[rows, :] = (num / den * (gate * _sigmoid(gate))).astype(o_ref.dtype)


def _attention(pa, pb, pc, bias, batch, seq, heads):
    n = pa.shape[0]
    nt = seq // ATT_TILE
    blk = (ATT_TILE, HEAD)

    def spec(k):
        return pl.BlockSpec(blk, lambda b, h, t, k=k: (b * nt + t, k * heads + h))

    scratch = []
    for _, d in DILATED_GROUPS:
        _, stride = _kv_layout(d)
        scratch += [pltpu.VMEM((d * stride, HEAD), bf16), pltpu.VMEM((d * stride, HEAD), bf16)]
    scratch += [pltpu.VMEM((N_GROUPS, ATT_TILE, HEAD), f32), pltpu.VMEM((N_GROUPS, ATT_TILE, HEAD), f32)]
    return pl.pallas_call(
        _attn_kernel,
        out_shape=jax.ShapeDtypeStruct((n, heads * HEAD), bf16),
        grid=(batch, heads, nt),
        in_specs=[spec(0), spec(1), spec(2), spec(3), spec(0), spec(1), spec(2), spec(0), spec(1), spec(2),
                  pl.BlockSpec((N_GROUPS, None, ATT_BLOCK, 2 * ATT_BLOCK), lambda b, h, t: (0, h, 0, 0))],
        out_specs=pl.BlockSpec(blk, lambda b, h, t: (b * nt + t, h)),
        scratch_shapes=scratch,
        compiler_params=_cparams(("parallel", "parallel", "arbitrary")),
        name="dilated_attention",
    )(pa, pa, pa, pa, pb, pb, pb, pc, pc, pc, bias)


def _t5_bucket(dist):
    max_exact = N_BUCKETS // 2
    df = jnp.maximum(dist, 1).astype(f32)
    large = max_exact + (jnp.log(df / max_exact) / math.log(MAX_DISTANCE / max_exact)
                         * (N_BUCKETS - max_exact)).astype(jnp.int32)
    large = jnp.minimum(large, N_BUCKETS - 1)
    return jnp.where(dist < max_exact, dist, large)


def _bias_tables(rel_bias, heads):
    a = jnp.arange(ATT_BLOCK)[:, None]
    c = jnp.arange(2 * ATT_BLOCK)[None, :]
    rel = ATT_BLOCK + a - c
    tables = []
    for g, (window, d) in enumerate(DILATED_GROUPS):
        assert window // d == ATT_BLOCK and ATT_TILE % (d * ATT_BLOCK) == 0
        band = (rel >= 0) & (rel <= window // d)
        bucket = _t5_bucket(jnp.maximum(rel, 0) * d)
        tab = rel_bias[:, g * heads:(g + 1) * heads].astype(f32)[bucket]
        tables.append(jnp.where(band[None], tab.transpose(2, 0, 1), NEG))
    return jnp.stack(tables, axis=0)


def _by_residue(u, d):
    n, k = u.shape
    return u.reshape(n // ATT_TILE, ATT_TILE // d, d, k).transpose(0, 2, 1, 3).reshape(n, k)


def kernel(x, ln_g, hg_w_in, hg_lb_logits, hg_norm_g, hg_w_out, att_w_in, att_w_out, rel_bias, final_g):
    batch, seq, d_model = x.shape
    n = batch * seq
    w = hg_w_out.shape[1]
    heads = w // HEAD
    assert seq % ATT_TILE == 0 and w % (2 * HEAD) == 0

    lower = jnp.cumsum(jax.nn.softmax(hg_lb_logits.astype(f32), axis=0), axis=0)
    h0 = x.reshape(n, d_model)

    u0 = _rmsnorm(h0, ln_g[0], bf16)
    p0 = _matmul(u0, hg_w_in[0].astype(bf16), f32, tm=1024, tn=1024)
    y0 = _hgrn(p0, lower[0], hg_norm_g[0], batch, seq, tt=1024)
    h1, u1 = _outproj(y0, hg_w_out[0].astype(bf16), h0, ln_g[1], bf16, emit_h=True)

    wa = att_w_in[0].astype(bf16)
    w_nat = jnp.concatenate([wa[:, :3 * w], wa[:, 3 * N_GROUPS * w:]], axis=1)
    pa = _matmul(u1, w_nat, bf16, tm=2048, tn=1024)
    pb = _matmul(_by_residue(u1, DILATED_GROUPS[1][1]), wa[:, 3 * w:6 * w], bf16, tm=2048, tn=1024)
    pc = _matmul(_by_residue(u1, DILATED_GROUPS[2][1]), wa[:, 6 * w:9 * w], bf16, tm=2048, tn=1024)
    y1 = _attention(pa, pb, pc, _bias_tables(rel_bias, heads), batch, seq, heads)
    (out,) = _outproj(y1, att_w_out[0].astype(bf16), h1, final_g, f32, emit_h=False)
    return out.reshape(batch, seq, d_model)
```

```python
import functools
import math

import jax
import jax.numpy as jnp
from jax import lax
from jax.experimental import pallas as pl
from jax.experimental.pallas import tpu as pltpu

EPS = 1e-6
HEAD = 128
HG_CHUNK = 64
ATT_BLOCK = 128
DILATED_GROUPS = ((128, 1), (512, 4), (2048, 16))
N_GROUPS = len(DILATED_GROUPS)
N_BUCKETS = 32
MAX_DISTANCE = 2048
ATT_TILE = ATT_BLOCK * max(d for _, d in DILATED_GROUPS)
NEG = -1e30

VMEM_LIMIT = 56 * 1024 * 1024

f32 = jnp.float32
bf16 = jnp.bfloat16


def _cparams(sem):
    return pltpu.CompilerParams(dimension_semantics=sem, vmem_limit_bytes=VMEM_LIMIT)


def _rmsnorm_kernel(x_ref, g_ref, o_ref):
    x = x_ref[...]
    ms = jnp.mean(x * x, axis=-1, keepdims=True)
    o_ref[...] = (x * lax.rsqrt(ms + EPS) * g_ref[...]).astype(o_ref.dtype)


def _rmsnorm(x, g, out_dtype, tm=1024):
    n, d = x.shape
    return pl.pallas_call(
        _rmsnorm_kernel,
        out_shape=jax.ShapeDtypeStruct((n, d), out_dtype),
        grid=(n // tm,),
        in_specs=[pl.BlockSpec((tm, d), lambda i: (i, 0)),
                  pl.BlockSpec((1, d), lambda i: (0, 0))],
        out_specs=pl.BlockSpec((tm, d), lambda i: (i, 0)),
        compiler_params=_cparams(("parallel",)),
        name="rmsnorm",
    )(x, g.reshape(1, d))


def _matmul_kernel(a_ref, w_ref, o_ref, *, row_chunk):
    tm = a_ref.shape[0]
    for r in range(tm // row_chunk):
        rows = slice(r * row_chunk, (r + 1) * row_chunk)
        o_ref[rows, :] = jnp.dot(a_ref[rows, :], w_ref[...], preferred_element_type=f32).astype(o_ref.dtype)


def _matmul(a, w, out_dtype, tm, tn):
    n, k = a.shape
    c = w.shape[1]
    tm = min(tm, n)
    tn = math.gcd(tn, c)
    return pl.pallas_call(
        functools.partial(_matmul_kernel, row_chunk=min(512, tm)),
        out_shape=jax.ShapeDtypeStruct((n, c), out_dtype),
        grid=(n // tm, c // tn),
        in_specs=[pl.BlockSpec((tm, k), lambda i, j: (i, 0)),
                  pl.BlockSpec((k, tn), lambda i, j: (0, j))],
        out_specs=pl.BlockSpec((tm, tn), lambda i, j: (i, j)),
        compiler_params=_cparams(("parallel", "arbitrary")),
        name="proj_matmul",
    )(a, w)


def _outproj_kernel(y_ref, w_ref, h_ref, g_ref, *out_refs, emit_h):
    h = h_ref[...] + jnp.dot(y_ref[...], w_ref[...], preferred_element_type=f32)
    if emit_h:
        out_refs[0][...] = h
    n_ref = out_refs[-1]
    ms = jnp.mean(h * h, axis=-1, keepdims=True)
    n_ref[...] = (h * lax.rsqrt(ms + EPS) * g_ref[...]).astype(n_ref.dtype)


def _outproj(y, w, h, g, norm_dtype, emit_h, tm=512):
    n, k = y.shape
    d = w.shape[1]
    tm = min(tm, n)
    row = lambda i: (i, 0)
    out_shape = [jax.ShapeDtypeStruct((n, d), norm_dtype)]
    out_specs = [pl.BlockSpec((tm, d), row)]
    if emit_h:
        out_shape.insert(0, jax.ShapeDtypeStruct((n, d), f32))
        out_specs.insert(0, pl.BlockSpec((tm, d), row))
    return pl.pallas_call(
        functools.partial(_outproj_kernel, emit_h=emit_h),
        out_shape=out_shape,
        grid=(n // tm,),
        in_specs=[pl.BlockSpec((tm, k), row),
                  pl.BlockSpec((k, d), lambda i: (0, 0)),
                  pl.BlockSpec((tm, d), row),
                  pl.BlockSpec((1, d), lambda i: (0, 0))],
        out_specs=out_specs,
        compiler_params=_cparams(("parallel",)),
        name="outproj",
    )(y, w, h, g.reshape(1, d))


def _sigmoid(x):
    return 1.0 / (1.0 + jnp.exp(-x))


def _hgrn_kernel(q_ref, f_ref, i_ref, g_ref, lb_ref, ng_ref, o_ref, st_ref, *, n_chunks):
    c = HG_CHUNK
    half = c // 2

    @pl.when(pl.program_id(2) == 0)
    def _():
        st_ref[...] = jnp.zeros_like(st_ref)

    lb = lb_ref[...]
    ng = ng_ref[...]
    row = lax.broadcasted_iota(jnp.int32, (c, c), 0)
    col = lax.broadcasted_iota(jnp.int32, (c, c), 1)
    causal = row >= col
    tril = causal.astype(f32)

    def chunk(ci, carry):
        rows = pl.ds(pl.multiple_of(ci * c, c), c)
        f = lb + (1.0 - lb) * _sigmoid(f_ref[rows, :])
        lf = jnp.log(f)
        k = 1.0 - f
        b = jnp.dot(tril, lf, precision=lax.Precision.HIGHEST, preferred_element_type=f32)
        b_mid = b[half - 1:half, :]
        b_last = b[c - 1:c, :]
        qm = q_ref[rows, :] * jnp.exp(b - b_mid)
        kd = (k * jnp.exp(b_mid - b)).astype(bf16)
        q0 = (qm * jnp.exp(b_mid)).astype(bf16)
        qd = qm.astype(bf16)
        v = i_ref[rows, :].astype(bf16)
        decay = jnp.exp(b_last)
        u_scale = jnp.exp(b_last - b_mid)
        gate = g_ref[rows, :]
        for h in range(2):
            ln = slice(h * HEAD, (h + 1) * HEAD)
            scores = lax.dot_general(qd[:, ln], kd[:, ln], (((1,), (1,)), ((), ())),
                                     preferred_element_type=f32)
            scores = jnp.where(causal, scores, 0.0).astype(bf16)
            st = st_ref[h]
            o = (jnp.dot(scores, v[:, ln], preferred_element_type=f32)
                 + lax.dot_general(q0[:, ln], st.astype(bf16), (((1,), (1,)), ((), ())),
                                   preferred_element_type=f32))
            upd = lax.dot_general(v[:, ln], kd[:, ln], (((0,), (0,)), ((), ())),
                                  preferred_element_type=f32)
            st_ref[h] = st * decay[:, ln] + upd * u_scale[:, ln]
            o = o * lax.rsqrt(jnp.mean(o * o, axis=-1, keepdims=True) + EPS)
            gh = gate[:, ln]
            o_ref[rows, ln] = (o * ng[:, ln] * (gh * _sigmoid(gh))).astype(o_ref.dtype)
        return carry

    lax.fori_loop(0, n_chunks, chunk, 0)


def _hgrn(p, lb, ng, batch, seq, tt):
    n, c4 = p.shape
    w = c4 // 4
    pairs = w // (2 * HEAD)
    tt = min(tt, seq)
    nt = seq // tt
    blk = (tt, 2 * HEAD)

    def spec(k):
        return pl.BlockSpec(blk, lambda b, pr, t, k=k: (b * nt + t, k * pairs + pr))

    vec = pl.BlockSpec((1, 2 * HEAD), lambda b, pr, t: (0, pr))
    return pl.pallas_call(
        functools.partial(_hgrn_kernel, n_chunks=tt // HG_CHUNK),
        out_shape=jax.ShapeDtypeStruct((n, w), bf16),
        grid=(batch, pairs, nt),
        in_specs=[spec(0), spec(1), spec(2), spec(3), vec, vec],
        out_specs=pl.BlockSpec(blk, lambda b, pr, t: (b * nt + t, pr)),
        scratch_shapes=[pltpu.VMEM((2, HEAD, HEAD), f32)],
        compiler_params=_cparams(("parallel", "parallel", "arbitrary")),
        name="hgrn2",
    )(p, p, p, p, lb.reshape(1, w), ng.reshape(1, w))


MERGE_D = 4
LOG2E = math.log2(math.e)


def _attn_kernel(q0_ref, k0_ref, v0_ref, gate_ref, q1_ref, k1_ref, v1_ref, q2_ref, k2_ref, v2_ref,
                 bias_ref, o_ref, ck0, cv0, ck1, cv1, ck2, cv2, oacc, lacc, macc, tmp, nat):
    t = pl.program_id(2)
    par = t % 2
    q_refs = (q0_ref, q1_ref, q2_ref)
    k_refs = (k0_ref, k1_ref, k2_ref)
    v_refs = (v0_ref, v1_ref, v2_ref)
    cks = (ck0, ck1, ck2)
    cvs = (cv0, cv1, cv2)
    scale2 = HEAD ** -0.5 * LOG2E
    blk = ATT_BLOCK
    res_rows = ATT_TILE // MERGE_D
    sub = blk // MERGE_D

    @pl.when(t == 0)
    def _():
        for g, (_, d) in enumerate(DILATED_GROUPS):
            cks[g][:, 2 * blk:, :] = jnp.zeros((d, blk, HEAD), bf16)
            cvs[g][:, 2 * blk:, :] = jnp.zeros((d, blk, HEAD), bf16)

    slot = pl.ds(pl.multiple_of(par * 2 * blk, blk), blk)
    for g, (_, d) in enumerate(DILATED_GROUPS):
        per_res = ATT_TILE // d
        for r in range(d):
            for src, dst in ((k_refs[g], cks[g]), (v_refs[g], cvs[g])):
                dst[r, blk:2 * blk, :] = src[r * per_res:r * per_res + blk, :]
                dst[r, slot, :] = src[(r + 1) * per_res - blk:(r + 1) * per_res, :]

    cwin = pl.ds(pl.multiple_of((1 - par) * blk, blk), 2 * blk)
    colid = lax.broadcasted_iota(jnp.int32, (blk, 2 * blk), 1)
    pen = jnp.where(colid >= blk, jnp.where(t == 0, NEG, 0.0).astype(f32), 0.0)

    def carry_bias(g):
        b = bias_ref[g]
        swapped = jnp.concatenate([b[:, blk:], b[:, :blk]], axis=1)
        return jnp.where(par == 1, b, swapped) + pen

    def attend(q, kwin, vwin, bias):
        s = lax.dot_general(q, kwin, (((1,), (1,)), ((), ())), preferred_element_type=f32)
        s = s * scale2 + bias
        m = jnp.max(s, axis=-1, keepdims=True)
        p = jnp.exp2(s - m)
        l = jnp.sum(p, axis=-1, keepdims=True)
        acc = jnp.dot(p.astype(bf16), vwin, preferred_element_type=f32)
        return acc, jnp.broadcast_to(l, (blk, HEAD)), jnp.broadcast_to(m, (blk, HEAD))

    def block(g, r, j):
        per_res = ATT_TILE // DILATED_GROUPS[g][1]
        q = q_refs[g][r * per_res + j * blk:r * per_res + (j + 1) * blk, :]
        if j == 0:
            return attend(q, cks[g][r, cwin, :], cvs[g][r, cwin, :], carry_bias(g))
        win = slice(r * per_res + (j - 1) * blk, r * per_res + (j + 1) * blk)
        return attend(q, k_refs[g][win, :], v_refs[g][win, :], bias_ref[g])

    dsts = (oacc, lacc, macc)

    for j in range(ATT_TILE // blk):
        for k, val in enumerate(block(0, 0, j)):
            tmp[3 * j + k] = val
            for r4 in range(MERGE_D):
                dsts[k][0, r4 * res_rows + j * sub:r4 * res_rows + (j + 1) * sub, :] = (
                    tmp[3 * j + k, pl.ds(r4, sub, stride=MERGE_D), :])

    d1 = DILATED_GROUPS[1][1]
    for r in range(d1):
        for j in range(ATT_TILE // d1 // blk):
            row0 = r * (ATT_TILE // d1) + j * blk
            for k, val in enumerate(block(1, r, j)):
                dsts[k][1, row0:row0 + blk, :] = val

    d2 = DILATED_GROUPS[2][1]
    for r16 in range(d2):
        rows = pl.ds((r16 % MERGE_D) * res_rows + r16 // MERGE_D, blk, stride=d2 // MERGE_D)
        for k, val in enumerate(block(2, r16, 0)):
            dsts[k][2, rows, :] = val

    def merge(c, carry):
        for r4 in range(MERGE_D):
            rows = pl.ds(pl.multiple_of(r4 * res_rows + c * blk, blk), blk)
            ms = [macc[g, rows, :] for g in range(N_GROUPS)]
            mx = functools.reduce(jnp.maximum, ms)
            ws = [jnp.exp2(x - mx) for x in ms]
            num = functools.reduce(lambda a, b: a + b, [w * oacc[g, rows, :] for g, w in enumerate(ws)])
            den = functools.reduce(lambda a, b: a + b, [w * lacc[g, rows, :] for g, w in enumerate(ws)])
            nat[pl.ds(c * blk * MERGE_D + r4, blk, stride=MERGE_D), :] = num / den
        return carry

    lax.fori_loop(0, res_rows // blk, merge, 0)

    rc = 256
    for c0 in range(0, ATT_TILE, rc):
        rows = slice(c0, c0 + rc)
        gate = gate_ref[rows, :].astype(f32)
        o_ref[rows, :] = (nat[rows, :] * (gate * _sigmoid(gate))).astype(o_ref.dtype)


def _attention(pa, pb, pc, bias, batch, seq, heads):
    n = pa.shape[0]
    nt = seq // ATT_TILE
    blk = (ATT_TILE, HEAD)

    def spec(k):
        return pl.BlockSpec(blk, lambda b, h, t, k=k: (b * nt + t, k * heads + h))

    scratch = []
    for _, d in DILATED_GROUPS:
        scratch += [pltpu.VMEM((d, 3 * ATT_BLOCK, HEAD), bf16)] * 2
    scratch += [pltpu.VMEM((N_GROUPS, ATT_TILE, HEAD), f32)] * 3
    scratch += [pltpu.VMEM((3 * ATT_TILE // ATT_BLOCK, ATT_BLOCK, HEAD), f32), pltpu.VMEM((ATT_TILE, HEAD), f32)]
    return pl.pallas_call(
        _attn_kernel,
        out_shape=jax.ShapeDtypeStruct((n, heads * HEAD), bf16),
        grid=(batch, heads, nt),
        in_specs=[spec(0), spec(1), spec(2), spec(3), spec(0), spec(1), spec(2), spec(0), spec(1), spec(2),
                  pl.BlockSpec((N_GROUPS, None, ATT_BLOCK, 2 * ATT_BLOCK), lambda b, h, t: (0, h, 0, 0))],
        out_specs=pl.BlockSpec(blk, lambda b, h, t: (b * nt + t, h)),
        scratch_shapes=scratch,
        compiler_params=_cparams(("parallel", "parallel", "arbitrary")),
        name="dilated_attention",
    )(pa, pa, pa, pa, pb, pb, pb, pc, pc, pc, bias)


def _t5_bucket(dist):
    max_exact = N_BUCKETS // 2
    df = jnp.maximum(dist, 1).astype(f32)
    large = max_exact + (jnp.log(df / max_exact) / math.log(MAX_DISTANCE / max_exact)
                         * (N_BUCKETS - max_exact)).astype(jnp.int32)
    large = jnp.minimum(large, N_BUCKETS - 1)
    return jnp.where(dist < max_exact, dist, large)


def _bias_tables(rel_bias, heads):
    a = jnp.arange(ATT_BLOCK)[:, None]
    c = jnp.arange(2 * ATT_BLOCK)[None, :]
    rel = ATT_BLOCK + a - c
    tables = []
    for g, (window, d) in enumerate(DILATED_GROUPS):
        assert window // d == ATT_BLOCK and ATT_TILE % (d * ATT_BLOCK) == 0
        band = (rel >= 0) & (rel <= window // d)
        bucket = _t5_bucket(jnp.maximum(rel, 0) * d)
        tab = rel_bias[:, g * heads:(g + 1) * heads].astype(f32)[bucket]
        tables.append(jnp.where(band[None], tab.transpose(2, 0, 1) * LOG2E, NEG))
    return jnp.stack(tables, axis=0)


def _by_residue(u, d):
    n, k = u.shape
    return u.reshape(n // ATT_TILE, ATT_TILE // d, d, k).transpose(0, 2, 1, 3).reshape(n, k)


def kernel(x, ln_g, hg_w_in, hg_lb_logits, hg_norm_g, hg_w_out, att_w_in, att_w_out, rel_bias, final_g):
    batch, seq, d_model = x.shape
    n = batch * seq
    w = hg_w_out.shape[1]
    heads = w // HEAD
    assert seq % ATT_TILE == 0 and w % (2 * HEAD) == 0

    lower = jnp.cumsum(jax.nn.softmax(hg_lb_logits.astype(f32), axis=0), axis=0)
    h0 = x.reshape(n, d_model)

    u0 = _rmsnorm(h0, ln_g[0], bf16)
    p0 = _matmul(u0, hg_w_in[0].astype(bf16), f32, tm=1024, tn=1024)
    y0 = _hgrn(p0, lower[0], hg_norm_g[0], batch, seq, tt=1024)
    h1, u1 = _outproj(y0, hg_w_out[0].astype(bf16), h0, ln_g[1], bf16, emit_h=True)

    wa = att_w_in[0].astype(bf16)
    w_nat = jnp.concatenate([wa[:, :3 * w], wa[:, 3 * N_GROUPS * w:]], axis=1)
    pa = _matmul(u1, w_nat, bf16, tm=2048, tn=1024)
    pb = _matmul(_by_residue(u1, DILATED_GROUPS[1][1]), wa[:, 3 * w:6 * w], bf16, tm=2048, tn=1024)
    pc = _matmul(_by_residue(u1, DILATED_GROUPS[2][1]), wa[:, 6 * w:9 * w], bf16, tm=2048, tn=1024)
    y1 = _attention(pa, pb, pc, _bias_tables(rel_bias, heads), batch, seq, heads)
    (out,) = _outproj(y1, att_w_out[0].astype(bf16), h1, final_g, f32, emit_h=False)
    return out.reshape(batch, seq, d_model)
```

```python
import functools
import math

import jax
import jax.numpy as jnp
from jax import lax
from jax.experimental import pallas as pl
from jax.experimental.pallas import tpu as pltpu

EPS = 1e-6
HEAD = 128
HG_CHUNK = 64
ATT_BLOCK = 128
DILATED_GROUPS = ((128, 1), (512, 4), (2048, 16))
N_GROUPS = len(DILATED_GROUPS)
N_BUCKETS = 32
MAX_DISTANCE = 2048
ATT_TILE = ATT_BLOCK * max(d for _, d in DILATED_GROUPS)
NEG = -1e30

VMEM_LIMIT = 56 * 1024 * 1024

f32 = jnp.float32
bf16 = jnp.bfloat16


def _cparams(sem):
    return pltpu.CompilerParams(dimension_semantics=sem, vmem_limit_bytes=VMEM_LIMIT)


def _rmsnorm_kernel(x_ref, g_ref, o_ref):
    x = x_ref[...]
    ms = jnp.mean(x * x, axis=-1, keepdims=True)
    o_ref[...] = (x * lax.rsqrt(ms + EPS) * g_ref[...]).astype(o_ref.dtype)


def _rmsnorm(x, g, out_dtype, tm=1024):
    n, d = x.shape
    return pl.pallas_call(
        _rmsnorm_kernel,
        out_shape=jax.ShapeDtypeStruct((n, d), out_dtype),
        grid=(n // tm,),
        in_specs=[pl.BlockSpec((tm, d), lambda i: (i, 0)),
                  pl.BlockSpec((1, d), lambda i: (0, 0))],
        out_specs=pl.BlockSpec((tm, d), lambda i: (i, 0)),
        compiler_params=_cparams(("parallel",)),
        name="rmsnorm",
    )(x, g.reshape(1, d))


def _matmul_kernel(a_ref, w_ref, o_ref, *, row_chunk):
    tm = a_ref.shape[0]
    for r in range(tm // row_chunk):
        rows = slice(r * row_chunk, (r + 1) * row_chunk)
        o_ref[rows, :] = jnp.dot(a_ref[rows, :], w_ref[...], preferred_element_type=f32).astype(o_ref.dtype)


def _matmul(a, w, out_dtype, tm, tn):
    n, k = a.shape
    c = w.shape[1]
    tm = min(tm, n)
    tn = math.gcd(tn, c)
    return pl.pallas_call(
        functools.partial(_matmul_kernel, row_chunk=min(512, tm)),
        out_shape=jax.ShapeDtypeStruct((n, c), out_dtype),
        grid=(n // tm, c // tn),
        in_specs=[pl.BlockSpec((tm, k), lambda i, j: (i, 0)),
                  pl.BlockSpec((k, tn), lambda i, j: (0, j))],
        out_specs=pl.BlockSpec((tm, tn), lambda i, j: (i, j)),
        compiler_params=_cparams(("parallel", "arbitrary")),
        name="proj_matmul",
    )(a, w)


def _outproj_kernel(y_ref, w_ref, h_ref, g_ref, *out_refs, emit_h):
    h = h_ref[...] + jnp.dot(y_ref[...], w_ref[...], preferred_element_type=f32)
    if emit_h:
        out_refs[0][...] = h
    n_ref = out_refs[-1]
    ms = jnp.mean(h * h, axis=-1, keepdims=True)
    n_ref[...] = (h * lax.rsqrt(ms + EPS) * g_ref[...]).astype(n_ref.dtype)


def _outproj(y, w, h, g, norm_dtype, emit_h, tm=512):
    n, k = y.shape
    d = w.shape[1]
    tm = min(tm, n)
    row = lambda i: (i, 0)
    out_shape = [jax.ShapeDtypeStruct((n, d), norm_dtype)]
    out_specs = [pl.BlockSpec((tm, d), row)]
    if emit_h:
        out_shape.insert(0, jax.ShapeDtypeStruct((n, d), f32))
        out_specs.insert(0, pl.BlockSpec((tm, d), row))
    return pl.pallas_call(
        functools.partial(_outproj_kernel, emit_h=emit_h),
        out_shape=out_shape,
        grid=(n // tm,),
        in_specs=[pl.BlockSpec((tm, k), row),
                  pl.BlockSpec((k, d), lambda i: (0, 0)),
                  pl.BlockSpec((tm, d), row),
                  pl.BlockSpec((1, d), lambda i: (0, 0))],
        out_specs=out_specs,
        compiler_params=_cparams(("parallel",)),
        name="outproj",
    )(y, w, h, g.reshape(1, d))


HG_GROUP = 4 * HG_CHUNK
HG_UNROLL = 2


def _sigmoid(x):
    return 1.0 / (1.0 + jnp.exp(-x))


def _split3(x):
    hi = x.astype(bf16)
    r1 = x - hi.astype(f32)
    mid = r1.astype(bf16)
    lo = (r1 - mid.astype(f32)).astype(bf16)
    return hi, mid, lo


def _hgrn_kernel(q_ref, f_ref, i_ref, g_ref, lb_ref, ng_ref, o_ref, st_ref, *, n_groups):
    c = HG_CHUNK
    half = c // 2
    rg = HG_GROUP
    nck = rg // c

    @pl.when(pl.program_id(2) == 0)
    def _():
        st_ref[...] = jnp.zeros_like(st_ref)

    lb = lb_ref[...]
    ng = ng_ref[...]
    row = lax.broadcasted_iota(jnp.int32, (rg, rg), 0)
    col = lax.broadcasted_iota(jnp.int32, (rg, rg), 1)
    causal = (row >= col) & ((row // c) == (col // c))
    tril = jnp.where(causal, 1.0, 0.0).astype(bf16)
    zero_blk = jnp.zeros((HEAD, HEAD), bf16)
    nt = (((1,), (1,)), ((), ()))
    tn = (((0,), (0,)), ((), ()))

    def decays(gi):
        rows = pl.ds(pl.multiple_of(gi * rg, rg), rg)
        f = lb + (1.0 - lb) * _sigmoid(f_ref[rows, :])
        b = functools.reduce(lambda x, y: x + y,
                             [jnp.dot(tril, part, preferred_element_type=f32) for part in _split3(jnp.log(f))])
        return rows, 1.0 - f, b

    def operands(rows, k, b):
        q = q_ref[rows, :]
        qd, kd, q0, decay, us = [], [], [], [], []
        for ci in range(nck):
            sl = slice(ci * c, (ci + 1) * c)
            bc = b[sl, :]
            b_mid = bc[half - 1:half, :]
            b_last = bc[c - 1:c, :]
            qm = q[sl, :] * jnp.exp(bc - b_mid)
            qd.append(qm.astype(bf16))
            kd.append((k[sl, :] * jnp.exp(b_mid - bc)).astype(bf16))
            q0.append((qm * jnp.exp(b_mid)).astype(bf16))
            decay.append(jnp.exp(b_last))
            us.append(jnp.exp(b_last - b_mid))
        return rows, qd, kd, q0, decay, us, i_ref[rows, :].astype(bf16)

    def intra_chunk(rows, qd, kd, q0, decay, us, v):
        qd_all = jnp.concatenate(qd, axis=0)
        kd_all = jnp.concatenate(kd, axis=0)
        intra = []
        for h in range(2):
            ln = slice(h * HEAD, (h + 1) * HEAD)
            s = lax.dot_general(qd_all[:, ln], kd_all[:, ln], nt, preferred_element_type=f32)
            s = jnp.where(causal, s, 0.0).astype(bf16)
            intra.append(jnp.dot(s, v[:, ln], preferred_element_type=f32))
        upd = [[lax.dot_general(v[ci * c:(ci + 1) * c, h * HEAD:(h + 1) * HEAD],
                                kd[ci][:, h * HEAD:(h + 1) * HEAD], tn, preferred_element_type=f32)
                * us[ci][:, h * HEAD:(h + 1) * HEAD] for h in range(2)] for ci in range(nck)]
        return rows, jnp.concatenate(intra, axis=1), upd, q0, decay

    def recur(gi, st, rows, o_intra, upd, q0, decay):
        gate = g_ref[rows, :]
        for ci in range(nck):
            sl = slice(ci * c, (ci + 1) * c)
            s0 = st[0].astype(bf16)
            s1 = st[1].astype(bf16)
            both = jnp.concatenate([jnp.concatenate([s0, zero_blk], axis=1),
                                    jnp.concatenate([zero_blk, s1], axis=1)], axis=0)
            o = o_intra[sl, :] + lax.dot_general(q0[ci], both, nt, preferred_element_type=f32)
            for h in range(2):
                ln = slice(h * HEAD, (h + 1) * HEAD)
                st[h] = st[h] * decay[ci][:, ln] + upd[ci][h]
                oh = o[:, ln]
                oh = oh * lax.rsqrt(jnp.mean(oh * oh, axis=-1, keepdims=True) + EPS)
                gh = gate[sl, ln]
                o_ref[pl.ds(pl.multiple_of(gi * rg + ci * c, c), c), ln] = (
                    oh * ng[:, ln] * (gh * _sigmoid(gh))).astype(o_ref.dtype)
        return st

    def group(it, carry):
        gis = [it * HG_UNROLL + u for u in range(HG_UNROLL)]
        stage = [decays(gi) for gi in gis]
        stage = [operands(*x) for x in stage]
        stage = [intra_chunk(*x) for x in stage]
        st = [st_ref[0], st_ref[1]]
        for gi, x in zip(gis, stage):
            st = recur(gi, st, *x)
        st_ref[0] = st[0]
        st_ref[1] = st[1]
        return carry

    lax.fori_loop(0, n_groups // HG_UNROLL, group, 0)


def _hgrn(p, lb, ng, batch, seq, tt):
    n, c4 = p.shape
    w = c4 // 4
    pairs = w // (2 * HEAD)
    tt = min(tt, seq)
    nt = seq // tt
    blk = (tt, 2 * HEAD)

    def spec(k):
        return pl.BlockSpec(blk, lambda b, pr, t, k=k: (b * nt + t, k * pairs + pr))

    vec = pl.BlockSpec((1, 2 * HEAD), lambda b, pr, t: (0, pr))
    return pl.pallas_call(
        functools.partial(_hgrn_kernel, n_groups=tt // HG_GROUP),
        out_shape=jax.ShapeDtypeStruct((n, w), bf16),
        grid=(batch, pairs, nt),
        in_specs=[spec(0), spec(1), spec(2), spec(3), vec, vec],
        out_specs=pl.BlockSpec(blk, lambda b, pr, t: (b * nt + t, pr)),
        scratch_shapes=[pltpu.VMEM((2, HEAD, HEAD), f32)],
        compiler_params=_cparams(("parallel", "parallel", "arbitrary")),
        name="hgrn2",
    )(p, p, p, p, lb.reshape(1, w), ng.reshape(1, w))


MERGE_D = 4
LOG2E = math.log2(math.e)


def _attn_kernel(q0_ref, k0_ref, v0_ref, gate_ref, q1_ref, k1_ref, v1_ref, q2_ref, k2_ref, v2_ref,
                 bias_ref, o_ref, ck0, cv0, ck1, cv1, ck2, cv2, oacc, lacc, macc, tmp, nat):
    t = pl.program_id(2)
    par = t % 2
    q_refs = (q0_ref, q1_ref, q2_ref)
    k_refs = (k0_ref, k1_ref, k2_ref)
    v_refs = (v0_ref, v1_ref, v2_ref)
    cks = (ck0, ck1, ck2)
    cvs = (cv0, cv1, cv2)
    scale2 = HEAD ** -0.5 * LOG2E
    blk = ATT_BLOCK
    res_rows = ATT_TILE // MERGE_D
    sub = blk // MERGE_D

    @pl.when(t == 0)
    def _():
        for g, (_, d) in enumerate(DILATED_GROUPS):
            cks[g][:, 2 * blk:, :] = jnp.zeros((d, blk, HEAD), bf16)
            cvs[g][:, 2 * blk:, :] = jnp.zeros((d, blk, HEAD), bf16)

    slot = pl.ds(pl.multiple_of(par * 2 * blk, blk), blk)
    for g, (_, d) in enumerate(DILATED_GROUPS):
        per_res = ATT_TILE // d
        for r in range(d):
            for src, dst in ((k_refs[g], cks[g]), (v_refs[g], cvs[g])):
                dst[r, blk:2 * blk, :] = src[r * per_res:r * per_res + blk, :]
                dst[r, slot, :] = src[(r + 1) * per_res - blk:(r + 1) * per_res, :]

    cwin = pl.ds(pl.multiple_of((1 - par) * blk, blk), 2 * blk)
    colid = lax.broadcasted_iota(jnp.int32, (blk, 2 * blk), 1)
    pen = jnp.where(colid >= blk, jnp.where(t == 0, NEG, 0.0).astype(f32), 0.0)

    def carry_bias(g):
        b = bias_ref[g]
        swapped = jnp.concatenate([b[:, blk:], b[:, :blk]], axis=1)
        return jnp.where(par == 1, b, swapped) + pen

    def attend(q, kwin, vwin, bias):
        s = lax.dot_general(q, kwin, (((1,), (1,)), ((), ())), preferred_element_type=f32)
        s = s * scale2 + bias
        m = jnp.max(s, axis=-1, keepdims=True)
        p = jnp.exp2(s - m)
        l = jnp.sum(p, axis=-1, keepdims=True)
        acc = jnp.dot(p.astype(bf16), vwin, preferred_element_type=f32)
        return acc, jnp.broadcast_to(l, (blk, HEAD)), jnp.broadcast_to(m, (blk, HEAD))

    def block(g, r, j):
        per_res = ATT_TILE // DILATED_GROUPS[g][1]
        q = q_refs[g][r * per_res + j * blk:r * per_res + (j + 1) * blk, :]
        if j == 0:
            return attend(q, cks[g][r, cwin, :], cvs[g][r, cwin, :], carry_bias(g))
        win = slice(r * per_res + (j - 1) * blk, r * per_res + (j + 1) * blk)
        return attend(q, k_refs[g][win, :], v_refs[g][win, :], bias_ref[g])

    dsts = (oacc, lacc, macc)

    for j in range(ATT_TILE // blk):
        for k, val in enumerate(block(0, 0, j)):
            tmp[3 * j + k] = val
            for r4 in range(MERGE_D):
                dsts[k][0, r4 * res_rows + j * sub:r4 * res_rows + (j + 1) * sub, :] = (
                    tmp[3 * j + k, pl.ds(r4, sub, stride=MERGE_D), :])

    d1 = DILATED_GROUPS[1][1]
    for r in range(d1):
        for j in range(ATT_TILE // d1 // blk):
            row0 = r * (ATT_TILE // d1) + j * blk
            for k, val in enumerate(block(1, r, j)):
                dsts[k][1, row0:row0 + blk, :] = val

    d2 = DILATED_GROUPS[2][1]
    for r16 in range(d2):
        rows = pl.ds((r16 % MERGE_D) * res_rows + r16 // MERGE_D, blk, stride=d2 // MERGE_D)
        for k, val in enumerate(block(2, r16, 0)):
            dsts[k][2, rows, :] = val

    def merge(c, carry):
        for r4 in range(MERGE_D):
            rows = pl.ds(pl.multiple_of(r4 * res_rows + c * blk, blk), blk)
            ms = [macc[g, rows, :] for g in range(N_GROUPS)]
            mx = functools.reduce(jnp.maximum, ms)
            ws = [jnp.exp2(x - mx) for x in ms]
            num = functools.reduce(lambda a, b: a + b, [w * oacc[g, rows, :] for g, w in enumerate(ws)])
            den = functools.reduce(lambda a, b: a + b, [w * lacc[g, rows, :] for g, w in enumerate(ws)])
            nat[pl.ds(c * blk * MERGE_D + r4, blk, stride=MERGE_D), :] = num / den
        return carry

    lax.fori_loop(0, res_rows // blk, merge, 0)

    rc = 256
    for c0 in range(0, ATT_TILE, rc):
        rows = slice(c0, c0 + rc)
        gate = gate_ref[rows, :].astype(f32)
        o_ref[rows, :] = (nat[rows, :] * (gate * _sigmoid(gate))).astype(o_ref.dtype)


def _attention(pa, pb, pc, bias, batch, seq, heads):
    n = pa.shape[0]
    nt = seq // ATT_TILE
    blk = (ATT_TILE, HEAD)

    def spec(k):
        return pl.BlockSpec(blk, lambda b, h, t, k=k: (b * nt + t, k * heads + h))

    scratch = []
    for _, d in DILATED_GROUPS:
        scratch += [pltpu.VMEM((d, 3 * ATT_BLOCK, HEAD), bf16)] * 2
    scratch += [pltpu.VMEM((N_GROUPS, ATT_TILE, HEAD), f32)] * 3
    scratch += [pltpu.VMEM((3 * ATT_TILE // ATT_BLOCK, ATT_BLOCK, HEAD), f32), pltpu.VMEM((ATT_TILE, HEAD), f32)]
    return pl.pallas_call(
        _attn_kernel,
        out_shape=jax.ShapeDtypeStruct((n, heads * HEAD), bf16),
        grid=(batch, heads, nt),
        in_specs=[spec(0), spec(1), spec(2), spec(3), spec(0), spec(1), spec(2), spec(0), spec(1), spec(2),
                  pl.BlockSpec((N_GROUPS, None, ATT_BLOCK, 2 * ATT_BLOCK), lambda b, h, t: (0, h, 0, 0))],
        out_specs=pl.BlockSpec(blk, lambda b, h, t: (b * nt + t, h)),
        scratch_shapes=scratch,
        compiler_params=_cparams(("parallel", "parallel", "arbitrary")),
        name="dilated_attention",
    )(pa, pa, pa, pa, pb, pb, pb, pc, pc, pc, bias)


def _t5_bucket(dist):
    max_exact = N_BUCKETS // 2
    df = jnp.maximum(dist, 1).astype(f32)
    large = max_exact + (jnp.log(df / max_exact) / math.log(MAX_DISTANCE / max_exact)
                         * (N_BUCKETS - max_exact)).astype(jnp.int32)
    large = jnp.minimum(large, N_BUCKETS - 1)
    return jnp.where(dist < max_exact, dist, large)


def _bias_tables(rel_bias, heads):
    a = jnp.arange(ATT_BLOCK)[:, None]
    c = jnp.arange(2 * ATT_BLOCK)[None, :]
    rel = ATT_BLOCK + a - c
    tables = []
    for g, (window, d) in enumerate(DILATED_GROUPS):
        assert window // d == ATT_BLOCK and ATT_TILE % (d * ATT_BLOCK) == 0
        band = (rel >= 0) & (rel <= window // d)
        bucket = _t5_bucket(jnp.maximum(rel, 0) * d)
        tab = rel_bias[:, g * heads:(g + 1) * heads].astype(f32)[bucket]
        tables.append(jnp.where(band[None], tab.transpose(2, 0, 1) * LOG2E, NEG))
    return jnp.stack(tables, axis=0)


def _by_residue(u, d):
    n, k = u.shape
    return u.reshape(n // ATT_TILE, ATT_TILE // d, d, k).transpose(0, 2, 1, 3).reshape(n, k)


def kernel(x, ln_g, hg_w_in, hg_lb_logits, hg_norm_g, hg_w_out, att_w_in, att_w_out, rel_bias, final_g):
    batch, seq, d_model = x.shape
    n = batch * seq
    w = hg_w_out.shape[1]
    heads = w // HEAD
    assert seq % ATT_TILE == 0 and w % (2 * HEAD) == 0

    lower = jnp.cumsum(jax.nn.softmax(hg_lb_logits.astype(f32), axis=0), axis=0)
    h0 = x.reshape(n, d_model)

    u0 = _rmsnorm(h0, ln_g[0], bf16)
    p0 = _matmul(u0, hg_w_in[0].astype(bf16), f32, tm=1024, tn=1024)
    y0 = _hgrn(p0, lower[0], hg_norm_g[0], batch, seq, tt=1024)
    h1, u1 = _outproj(y0, hg_w_out[0].astype(bf16), h0, ln_g[1], bf16, emit_h=True)

    wa = att_w_in[0].astype(bf16)
    w_nat = jnp.concatenate([wa[:, :3 * w], wa[:, 3 * N_GROUPS * w:]], axis=1)
    pa = _matmul(u1, w_nat, bf16, tm=2048, tn=1024)
    pb = _matmul(_by_residue(u1, DILATED_GROUPS[1][1]), wa[:, 3 * w:6 * w], bf16, tm=2048, tn=1024)
    pc = _matmul(_by_residue(u1, DILATED_GROUPS[2][1]), wa[:, 6 * w:9 * w], bf16, tm=2048, tn=1024)
    y1 = _attention(pa, pb, pc, _bias_tables(rel_bias, heads), batch, seq, heads)
    (out,) = _outproj(y1, att_w_out[0].astype(bf16), h1, final_g, f32, emit_h=False)
    return out.reshape(batch, seq, d_model)
```

```python
import functools
import math

import jax
import jax.numpy as jnp
from jax import lax
from jax.experimental import pallas as pl
from jax.experimental.pallas import tpu as pltpu

EPS = 1e-6
HEAD = 128
HG_CHUNK = 64
ATT_BLOCK = 128
DILATED_GROUPS = ((128, 1), (512, 4), (2048, 16))
N_GROUPS = len(DILATED_GROUPS)
N_BUCKETS = 32
MAX_DISTANCE = 2048
ATT_TILE = ATT_BLOCK * max(d for _, d in DILATED_GROUPS)
NEG = -1e30

VMEM_LIMIT = 56 * 1024 * 1024

f32 = jnp.float32
bf16 = jnp.bfloat16


def _cparams(sem):
    return pltpu.CompilerParams(dimension_semantics=sem, vmem_limit_bytes=VMEM_LIMIT)


def _rmsnorm_kernel(x_ref, g_ref, o_ref):
    x = x_ref[...]
    ms = jnp.mean(x * x, axis=-1, keepdims=True)
    o_ref[...] = (x * lax.rsqrt(ms + EPS) * g_ref[...]).astype(o_ref.dtype)


def _rmsnorm(x, g, out_dtype, tm=1024):
    n, d = x.shape
    return pl.pallas_call(
        _rmsnorm_kernel,
        out_shape=jax.ShapeDtypeStruct((n, d), out_dtype),
        grid=(n // tm,),
        in_specs=[pl.BlockSpec((tm, d), lambda i: (i, 0)),
                  pl.BlockSpec((1, d), lambda i: (0, 0))],
        out_specs=pl.BlockSpec((tm, d), lambda i: (i, 0)),
        compiler_params=_cparams(("parallel",)),
        name="rmsnorm",
    )(x, g.reshape(1, d))


def _matmul_kernel(a_ref, w_ref, o_ref, *, row_chunk):
    tm = a_ref.shape[0]
    for r in range(tm // row_chunk):
        rows = slice(r * row_chunk, (r + 1) * row_chunk)
        o_ref[rows, :] = jnp.dot(a_ref[rows, :], w_ref[...], preferred_element_type=f32).astype(o_ref.dtype)


def _matmul(a, w, out_dtype, tm, tn):
    n, k = a.shape
    c = w.shape[1]
    tm = min(tm, n)
    tn = math.gcd(tn, c)
    return pl.pallas_call(
        functools.partial(_matmul_kernel, row_chunk=min(512, tm)),
        out_shape=jax.ShapeDtypeStruct((n, c), out_dtype),
        grid=(n // tm, c // tn),
        in_specs=[pl.BlockSpec((tm, k), lambda i, j: (i, 0)),
                  pl.BlockSpec((k, tn), lambda i, j: (0, j))],
        out_specs=pl.BlockSpec((tm, tn), lambda i, j: (i, j)),
        compiler_params=_cparams(("parallel", "arbitrary")),
        name="proj_matmul",
    )(a, w)


def _outproj_kernel(y_ref, w_ref, h_ref, g_ref, *out_refs, emit_h):
    h = h_ref[...] + jnp.dot(y_ref[...], w_ref[...], preferred_element_type=f32)
    if emit_h:
        out_refs[0][...] = h
    n_ref = out_refs[-1]
    ms = jnp.mean(h * h, axis=-1, keepdims=True)
    n_ref[...] = (h * lax.rsqrt(ms + EPS) * g_ref[...]).astype(n_ref.dtype)


def _outproj(y, w, h, g, norm_dtype, emit_h, tm=512):
    n, k = y.shape
    d = w.shape[1]
    tm = min(tm, n)
    row = lambda i: (i, 0)
    out_shape = [jax.ShapeDtypeStruct((n, d), norm_dtype)]
    out_specs = [pl.BlockSpec((tm, d), row)]
    if emit_h:
        out_shape.insert(0, jax.ShapeDtypeStruct((n, d), f32))
        out_specs.insert(0, pl.BlockSpec((tm, d), row))
    return pl.pallas_call(
        functools.partial(_outproj_kernel, emit_h=emit_h),
        out_shape=out_shape,
        grid=(n // tm,),
        in_specs=[pl.BlockSpec((tm, k), row),
                  pl.BlockSpec((k, d), lambda i: (0, 0)),
                  pl.BlockSpec((tm, d), row),
                  pl.BlockSpec((1, d), lambda i: (0, 0))],
        out_specs=out_specs,
        compiler_params=_cparams(("parallel",)),
        name="outproj",
    )(y, w, h, g.reshape(1, d))


HG_GROUP = 4 * HG_CHUNK
HG_UNROLL = 2


def _sigmoid(x):
    return 1.0 / (1.0 + jnp.exp(-x))


def _split3(x):
    hi = x.astype(bf16)
    r1 = x - hi.astype(f32)
    mid = r1.astype(bf16)
    lo = (r1 - mid.astype(f32)).astype(bf16)
    return hi, mid, lo


def _hgrn_kernel(q_ref, f_ref, i_ref, g_ref, lb_ref, ng_ref, o_ref, st_ref, *, n_groups):
    c = HG_CHUNK
    half = c // 2
    rg = HG_GROUP
    nck = rg // c

    @pl.when(pl.program_id(2) == 0)
    def _():
        st_ref[...] = jnp.zeros_like(st_ref)

    lb = lb_ref[...]
    ng = ng_ref[...]
    row = lax.broadcasted_iota(jnp.int32, (rg, rg), 0)
    col = lax.broadcasted_iota(jnp.int32, (rg, rg), 1)
    causal = (row >= col) & ((row // c) == (col // c))
    tril = jnp.where(causal, 1.0, 0.0).astype(bf16)
    zero_blk = jnp.zeros((HEAD, HEAD), bf16)
    nt = (((1,), (1,)), ((), ()))
    tn = (((0,), (0,)), ((), ()))

    def decays(gi):
        rows = pl.ds(pl.multiple_of(gi * rg, rg), rg)
        f = lb + (1.0 - lb) * _sigmoid(f_ref[rows, :])
        b = functools.reduce(lambda x, y: x + y,
                             [jnp.dot(tril, part, preferred_element_type=f32) for part in _split3(jnp.log(f))])
        return rows, 1.0 - f, b

    def operands(rows, k, b):
        q = q_ref[rows, :].astype(f32)
        qd, kd, q0, decay, us = [], [], [], [], []
        for ci in range(nck):
            sl = slice(ci * c, (ci + 1) * c)
            bc = b[sl, :]
            b_mid = bc[half - 1:half, :]
            b_last = bc[c - 1:c, :]
            qm = q[sl, :] * jnp.exp(bc - b_mid)
            qd.append(qm.astype(bf16))
            kd.append((k[sl, :] * jnp.exp(b_mid - bc)).astype(bf16))
            q0.append((qm * jnp.exp(b_mid)).astype(bf16))
            decay.append(jnp.exp(b_last))
            us.append(jnp.exp(b_last - b_mid))
        return rows, qd, kd, q0, decay, us, i_ref[rows, :]

    def intra_chunk(rows, qd, kd, q0, decay, us, v):
        qd_all = jnp.concatenate(qd, axis=0)
        kd_all = jnp.concatenate(kd, axis=0)
        intra = []
        for h in range(2):
            ln = slice(h * HEAD, (h + 1) * HEAD)
            s = lax.dot_general(qd_all[:, ln], kd_all[:, ln], nt, preferred_element_type=f32)
            s = jnp.where(causal, s, 0.0).astype(bf16)
            intra.append(jnp.dot(s, v[:, ln], preferred_element_type=f32))
        upd = [[lax.dot_general(v[ci * c:(ci + 1) * c, h * HEAD:(h + 1) * HEAD],
                                kd[ci][:, h * HEAD:(h + 1) * HEAD], tn, preferred_element_type=f32)
                * us[ci][:, h * HEAD:(h + 1) * HEAD] for h in range(2)] for ci in range(nck)]
        return rows, jnp.concatenate(intra, axis=1), upd, q0, decay

    def recur(gi, st, rows, o_intra, upd, q0, decay):
        gate = g_ref[rows, :].astype(f32)
        for ci in range(nck):
            sl = slice(ci * c, (ci + 1) * c)
            s0 = st[0].astype(bf16)
            s1 = st[1].astype(bf16)
            both = jnp.concatenate([jnp.concatenate([s0, zero_blk], axis=1),
                                    jnp.concatenate([zero_blk, s1], axis=1)], axis=0)
            o = o_intra[sl, :] + lax.dot_general(q0[ci], both, nt, preferred_element_type=f32)
            for h in range(2):
                ln = slice(h * HEAD, (h + 1) * HEAD)
                st[h] = st[h] * decay[ci][:, ln] + upd[ci][h]
                oh = o[:, ln]
                oh = oh * lax.rsqrt(jnp.mean(oh * oh, axis=-1, keepdims=True) + EPS)
                gh = gate[sl, ln]
                o_ref[pl.ds(pl.multiple_of(gi * rg + ci * c, c), c), ln] = (
                    oh * ng[:, ln] * (gh * _sigmoid(gh))).astype(o_ref.dtype)
        return st

    def group(it, carry):
        gis = [it * HG_UNROLL + u for u in range(HG_UNROLL)]
        stage = [decays(gi) for gi in gis]
        stage = [operands(*x) for x in stage]
        stage = [intra_chunk(*x) for x in stage]
        st = [st_ref[0], st_ref[1]]
        for gi, x in zip(gis, stage):
            st = recur(gi, st, *x)
        st_ref[0] = st[0]
        st_ref[1] = st[1]
        return carry

    lax.fori_loop(0, n_groups // HG_UNROLL, group, 0)


def _hgrn(pq, pf, lb, ng, batch, seq, tt):
    n, w = pf.shape
    pairs = w // (2 * HEAD)
    tt = min(tt, seq)
    nt = seq // tt
    blk = (tt, 2 * HEAD)

    def spec(k):
        return pl.BlockSpec(blk, lambda b, pr, t, k=k: (b * nt + t, k * pairs + pr))

    vec = pl.BlockSpec((1, 2 * HEAD), lambda b, pr, t: (0, pr))
    return pl.pallas_call(
        functools.partial(_hgrn_kernel, n_groups=tt // HG_GROUP),
        out_shape=jax.ShapeDtypeStruct((n, w), bf16),
        grid=(batch, pairs, nt),
        in_specs=[spec(0), spec(0), spec(1), spec(2), vec, vec],
        out_specs=pl.BlockSpec(blk, lambda b, pr, t: (b * nt + t, pr)),
        scratch_shapes=[pltpu.VMEM((2, HEAD, HEAD), f32)],
        compiler_params=_cparams(("parallel", "parallel", "arbitrary")),
        name="hgrn2",
    )(pq, pf, pq, pq, lb.reshape(1, w), ng.reshape(1, w))


MERGE_D = 4
LOG2E = math.log2(math.e)


def _attn_kernel(q0_ref, k0_ref, v0_ref, gate_ref, q1_ref, k1_ref, v1_ref, q2_ref, k2_ref, v2_ref,
                 bias_ref, o_ref, ck0, cv0, ck1, cv1, ck2, cv2, oacc, lacc, macc, tmp, nat):
    t = pl.program_id(2)
    par = t % 2
    q_refs = (q0_ref, q1_ref, q2_ref)
    k_refs = (k0_ref, k1_ref, k2_ref)
    v_refs = (v0_ref, v1_ref, v2_ref)
    cks = (ck0, ck1, ck2)
    cvs = (cv0, cv1, cv2)
    scale2 = HEAD ** -0.5 * LOG2E
    blk = ATT_BLOCK
    res_rows = ATT_TILE // MERGE_D
    sub = blk // MERGE_D

    @pl.when(t == 0)
    def _():
        for g, (_, d) in enumerate(DILATED_GROUPS):
            cks[g][:, 2 * blk:, :] = jnp.zeros((d, blk, HEAD), bf16)
            cvs[g][:, 2 * blk:, :] = jnp.zeros((d, blk, HEAD), bf16)

    slot = pl.ds(pl.multiple_of(par * 2 * blk, blk), blk)
    for g, (_, d) in enumerate(DILATED_GROUPS):
        per_res = ATT_TILE // d
        for r in range(d):
            for src, dst in ((k_refs[g], cks[g]), (v_refs[g], cvs[g])):
                dst[r, blk:2 * blk, :] = src[r * per_res:r * per_res + blk, :]
                dst[r, slot, :] = src[(r + 1) * per_res - blk:(r + 1) * per_res, :]

    cwin = pl.ds(pl.multiple_of((1 - par) * blk, blk), 2 * blk)
    colid = lax.broadcasted_iota(jnp.int32, (blk, 2 * blk), 1)
    pen = jnp.where(colid >= blk, jnp.where(t == 0, NEG, 0.0).astype(f32), 0.0)

    def carry_bias(g):
        b = bias_ref[g]
        swapped = jnp.concatenate([b[:, blk:], b[:, :blk]], axis=1)
        return jnp.where(par == 1, b, swapped) + pen

    def attend(q, kwin, vwin, bias):
        s = lax.dot_general(q, kwin, (((1,), (1,)), ((), ())), preferred_element_type=f32)
        s = s * scale2 + bias
        m = jnp.max(s, axis=-1, keepdims=True)
        p = jnp.exp2(s - m)
        l = jnp.sum(p, axis=-1, keepdims=True)
        acc = jnp.dot(p.astype(bf16), vwin, preferred_element_type=f32)
        return acc, jnp.broadcast_to(l, (blk, HEAD)), jnp.broadcast_to(m, (blk, HEAD))

    def block(g, r, j):
        per_res = ATT_TILE // DILATED_GROUPS[g][1]
        q = q_refs[g][r * per_res + j * blk:r * per_res + (j + 1) * blk, :]
        if j == 0:
            return attend(q, cks[g][r, cwin, :], cvs[g][r, cwin, :], carry_bias(g))
        win = slice(r * per_res + (j - 1) * blk, r * per_res + (j + 1) * blk)
        return attend(q, k_refs[g][win, :], v_refs[g][win, :], bias_ref[g])

    dsts = (oacc, lacc, macc)

    for j in range(ATT_TILE // blk):
        for k, val in enumerate(block(0, 0, j)):
            tmp[3 * j + k] = val
            for r4 in range(MERGE_D):
                dsts[k][0, r4 * res_rows + j * sub:r4 * res_rows + (j + 1) * sub, :] = (
                    tmp[3 * j + k, pl.ds(r4, sub, stride=MERGE_D), :])

    d1 = DILATED_GROUPS[1][1]
    for r in range(d1):
        for j in range(ATT_TILE // d1 // blk):
            row0 = r * (ATT_TILE // d1) + j * blk
            for k, val in enumerate(block(1, r, j)):
                dsts[k][1, row0:row0 + blk, :] = val

    d2 = DILATED_GROUPS[2][1]
    for r16 in range(d2):
        rows = pl.ds((r16 % MERGE_D) * res_rows + r16 // MERGE_D, blk, stride=d2 // MERGE_D)
        for k, val in enumerate(block(2, r16, 0)):
            dsts[k][2, rows, :] = val

    def merge(c, carry):
        for r4 in range(MERGE_D):
            rows = pl.ds(pl.multiple_of(r4 * res_rows + c * blk, blk), blk)
            ms = [macc[g, rows, :] for g in range(N_GROUPS)]
            mx = functools.reduce(jnp.maximum, ms)
            ws = [jnp.exp2(x - mx) for x in ms]
            num = functools.reduce(lambda a, b: a + b, [w * oacc[g, rows, :] for g, w in enumerate(ws)])
            den = functools.reduce(lambda a, b: a + b, [w * lacc[g, rows, :] for g, w in enumerate(ws)])
            nat[pl.ds(c * blk * MERGE_D + r4, blk, stride=MERGE_D), :] = num / den
        return carry

    lax.fori_loop(0, res_rows // blk, merge, 0)

    rc = 256
    for c0 in range(0, ATT_TILE, rc):
        rows = slice(c0, c0 + rc)
        gate = gate_ref[rows, :].astype(f32)
        o_ref[rows, :] = (nat[rows, :] * (gate * _sigmoid(gate))).astype(o_ref.dtype)


def _attention(p, bias, batch, seq, heads):
    n = p.shape[0]
    nt = seq // ATT_TILE
    blk = (ATT_TILE, HEAD)

    def spec(k):
        return pl.BlockSpec(blk, lambda b, h, t, k=k: (b * nt + t, k * heads + h))

    scratch = []
    for _, d in DILATED_GROUPS:
        scratch += [pltpu.VMEM((d, 3 * ATT_BLOCK, HEAD), bf16)] * 2
    scratch += [pltpu.VMEM((N_GROUPS, ATT_TILE, HEAD), f32)] * 3
    scratch += [pltpu.VMEM((3 * ATT_TILE // ATT_BLOCK, ATT_BLOCK, HEAD), f32), pltpu.VMEM((ATT_TILE, HEAD), f32)]
    return pl.pallas_call(
        _attn_kernel,
        out_shape=jax.ShapeDtypeStruct((n, heads * HEAD), bf16),
        grid=(batch, heads, nt),
        in_specs=[spec(0), spec(1), spec(2), spec(3 * N_GROUPS), spec(3), spec(4), spec(5), spec(6), spec(7), spec(8),
                  pl.BlockSpec((N_GROUPS, None, ATT_BLOCK, 2 * ATT_BLOCK), lambda b, h, t: (0, h, 0, 0))],
        out_specs=pl.BlockSpec(blk, lambda b, h, t: (b * nt + t, h)),
        scratch_shapes=scratch,
        compiler_params=_cparams(("parallel", "parallel", "arbitrary")),
        name="dilated_attention",
    )(*([p] * 10), bias)


def _t5_bucket(dist):
    max_exact = N_BUCKETS // 2
    df = jnp.maximum(dist, 1).astype(f32)
    large = max_exact + (jnp.log(df / max_exact) / math.log(MAX_DISTANCE / max_exact)
                         * (N_BUCKETS - max_exact)).astype(jnp.int32)
    large = jnp.minimum(large, N_BUCKETS - 1)
    return jnp.where(dist < max_exact, dist, large)


def _bias_tables(rel_bias, heads):
    a = jnp.arange(ATT_BLOCK)[:, None]
    c = jnp.arange(2 * ATT_BLOCK)[None, :]
    rel = ATT_BLOCK + a - c
    tables = []
    for g, (window, d) in enumerate(DILATED_GROUPS):
        assert window // d == ATT_BLOCK and ATT_TILE % (d * ATT_BLOCK) == 0
        band = (rel >= 0) & (rel <= window // d)
        onehot = jax.nn.one_hot(_t5_bucket(jnp.maximum(rel, 0) * d), N_BUCKETS, dtype=f32)
        tab = jnp.einsum("acb,bh->hac", onehot, rel_bias[:, g * heads:(g + 1) * heads].astype(f32),
                         precision=lax.Precision.HIGHEST)
        tables.append(jnp.where(band[None], tab * LOG2E, NEG))
    return jnp.stack(tables, axis=0)


PERM_ROWS = 256


def _groupproj_kernel(a_ref, w_ref, o_ref, perm_ref, *, col_blocks_per_group):
    j = pl.program_id(1)
    tm = a_ref.shape[0]

    @pl.when(j == 0)
    def _():
        i = lax.broadcasted_iota(jnp.int32, (PERM_ROWS, PERM_ROWS), 0)
        k = lax.broadcasted_iota(jnp.int32, (PERM_ROWS, PERM_ROWS), 1)
        for g, (_, d) in enumerate(DILATED_GROUPS[1:]):
            per = PERM_ROWS // d
            perm = jnp.where(k == (i % per) * d + i // per, 1.0, 0.0).astype(bf16)
            for sb in range(tm // PERM_ROWS):
                y = jnp.dot(perm, a_ref[sb * PERM_ROWS:(sb + 1) * PERM_ROWS, :],
                            preferred_element_type=f32).astype(bf16)
                for r in range(d):
                    dst = r * (tm // d) + sb * per
                    perm_ref[g, dst:dst + per, :] = y[r * per:(r + 1) * per, :]

    def project(lhs_ref):
        rc = 512
        for r in range(tm // rc):
            rows = slice(r * rc, (r + 1) * rc)
            o_ref[rows, :] = jnp.dot(lhs_ref[rows, :], w_ref[...], preferred_element_type=f32).astype(o_ref.dtype)

    grp = j // col_blocks_per_group
    for g in range(1, N_GROUPS):
        pl.when(grp == g)(functools.partial(project, perm_ref.at[g - 1]))
    pl.when((grp == 0) | (grp >= N_GROUPS))(functools.partial(project, a_ref))


def _groupproj(u, w, width, tn=1024):
    n, k = u.shape
    c = w.shape[1]
    tn = math.gcd(tn, width)
    return pl.pallas_call(
        functools.partial(_groupproj_kernel, col_blocks_per_group=3 * width // tn),
        out_shape=jax.ShapeDtypeStruct((n, c), bf16),
        grid=(n // ATT_TILE, c // tn),
        in_specs=[pl.BlockSpec((ATT_TILE, k), lambda i, j: (i, 0)),
                  pl.BlockSpec((k, tn), lambda i, j: (0, j))],
        out_specs=pl.BlockSpec((ATT_TILE, tn), lambda i, j: (i, j)),
        scratch_shapes=[pltpu.VMEM((N_GROUPS - 1, ATT_TILE, k), bf16)],
        compiler_params=_cparams(("parallel", "arbitrary")),
        name="group_proj",
    )(u, w)


def kernel(x, ln_g, hg_w_in, hg_lb_logits, hg_norm_g, hg_w_out, att_w_in, att_w_out, rel_bias, final_g):
    batch, seq, d_model = x.shape
    n = batch * seq
    w = hg_w_out.shape[1]
    heads = w // HEAD
    assert seq % ATT_TILE == 0 and w % (2 * HEAD) == 0

    lower = jnp.cumsum(jax.nn.softmax(hg_lb_logits.astype(f32), axis=0), axis=0)
    h0 = x.reshape(n, d_model)

    u0 = _rmsnorm(h0, ln_g[0], bf16)
    w0 = hg_w_in[0].astype(bf16)
    pq = _matmul(u0, jnp.concatenate([w0[:, :w], w0[:, 2 * w:]], axis=1), bf16, tm=1024, tn=1024)
    pf = _matmul(u0, w0[:, w:2 * w], f32, tm=1024, tn=1024)
    y0 = _hgrn(pq, pf, lower[0], hg_norm_g[0], batch, seq, tt=1024)
    h1, u1 = _outproj(y0, hg_w_out[0].astype(bf16), h0, ln_g[1], bf16, emit_h=True)

    p1 = _groupproj(u1, att_w_in[0].astype(bf16), w)
    y1 = _attention(p1, _bias_tables(rel_bias, heads), batch, seq, heads)
    (out,) = _outproj(y1, att_w_out[0].astype(bf16), h1, final_g, f32, emit_h=False)
    return out.reshape(batch, seq, d_model)
```

```python
import functools
import math

import jax
import jax.numpy as jnp
from jax import lax
from jax.experimental import pallas as pl
from jax.experimental.pallas import tpu as pltpu

EPS = 1e-6
HEAD = 128
HG_CHUNK = 64
ATT_BLOCK = 128
DILATED_GROUPS = ((128, 1), (512, 4), (2048, 16))
N_GROUPS = len(DILATED_GROUPS)
N_BUCKETS = 32
MAX_DISTANCE = 2048
ATT_TILE = ATT_BLOCK * max(d for _, d in DILATED_GROUPS)
NEG = -1e30

VMEM_LIMIT = 56 * 1024 * 1024

f32 = jnp.float32
bf16 = jnp.bfloat16


def _cparams(sem):
    return pltpu.CompilerParams(dimension_semantics=sem, vmem_limit_bytes=VMEM_LIMIT)


def _rmsnorm_kernel(x_ref, g_ref, o_ref):
    x = x_ref[...]
    ms = jnp.mean(x * x, axis=-1, keepdims=True)
    o_ref[...] = (x * lax.rsqrt(ms + EPS) * g_ref[...]).astype(o_ref.dtype)


def _rmsnorm(x, g, out_dtype, tm=1024):
    n, d = x.shape
    return pl.pallas_call(
        _rmsnorm_kernel,
        out_shape=jax.ShapeDtypeStruct((n, d), out_dtype),
        grid=(n // tm,),
        in_specs=[pl.BlockSpec((tm, d), lambda i: (i, 0)),
                  pl.BlockSpec((1, d), lambda i: (0, 0))],
        out_specs=pl.BlockSpec((tm, d), lambda i: (i, 0)),
        compiler_params=_cparams(("parallel",)),
        name="rmsnorm",
    )(x, g.reshape(1, d))


def _matmul_kernel(a_ref, w_ref, o_ref, *, row_chunk):
    tm = a_ref.shape[0]
    for r in range(tm // row_chunk):
        rows = slice(r * row_chunk, (r + 1) * row_chunk)
        o_ref[rows, :] = jnp.dot(a_ref[rows, :], w_ref[...], preferred_element_type=f32).astype(o_ref.dtype)


def _matmul(a, w, out_dtype, tm, tn):
    n, k = a.shape
    c = w.shape[1]
    tm = min(tm, n)
    tn = math.gcd(tn, c)
    return pl.pallas_call(
        functools.partial(_matmul_kernel, row_chunk=min(512, tm)),
        out_shape=jax.ShapeDtypeStruct((n, c), out_dtype),
        grid=(n // tm, c // tn),
        in_specs=[pl.BlockSpec((tm, k), lambda i, j: (i, 0)),
                  pl.BlockSpec((k, tn), lambda i, j: (0, j))],
        out_specs=pl.BlockSpec((tm, tn), lambda i, j: (i, j)),
        compiler_params=_cparams(("parallel", "arbitrary")),
        name="proj_matmul",
    )(a, w)


def _outproj_kernel(y_ref, w_ref, h_ref, g_ref, *out_refs, emit_h):
    h = h_ref[...] + jnp.dot(y_ref[...], w_ref[...], preferred_element_type=f32)
    if emit_h:
        out_refs[0][...] = h
    n_ref = out_refs[-1]
    ms = jnp.mean(h * h, axis=-1, keepdims=True)
    n_ref[...] = (h * lax.rsqrt(ms + EPS) * g_ref[...]).astype(n_ref.dtype)


def _outproj(y, w, h, g, norm_dtype, emit_h, tm=512):
    n, k = y.shape
    d = w.shape[1]
    tm = min(tm, n)
    row = lambda i: (i, 0)
    out_shape = [jax.ShapeDtypeStruct((n, d), norm_dtype)]
    out_specs = [pl.BlockSpec((tm, d), row)]
    if emit_h:
        out_shape.insert(0, jax.ShapeDtypeStruct((n, d), f32))
        out_specs.insert(0, pl.BlockSpec((tm, d), row))
    return pl.pallas_call(
        functools.partial(_outproj_kernel, emit_h=emit_h),
        out_shape=out_shape,
        grid=(n // tm,),
        in_specs=[pl.BlockSpec((tm, k), row),
                  pl.BlockSpec((k, d), lambda i: (0, 0)),
                  pl.BlockSpec((tm, d), row),
                  pl.BlockSpec((1, d), lambda i: (0, 0))],
        out_specs=out_specs,
        compiler_params=_cparams(("parallel",)),
        name="outproj",
    )(y, w, h, g.reshape(1, d))


HG_GROUP = 4 * HG_CHUNK
HG_BLOCK = 2 * HG_GROUP
HG_TILE = 2 * HG_BLOCK


def _sigmoid(x):
    return 1.0 / (1.0 + jnp.exp(-x))


def _split3(x):
    hi = x.astype(bf16)
    r1 = x - hi.astype(f32)
    mid = r1.astype(bf16)
    lo = (r1 - mid.astype(f32)).astype(bf16)
    return hi, mid, lo


def _hgrn_kernel(u_ref, un_ref, wq_ref, wf_ref, wi_ref, wg_ref, lb_ref, ng_ref, o_ref, st_ref, ps_ref):
    c = HG_CHUNK
    half = c // 2
    rg = HG_GROUP
    nck = rg // c
    w_refs = (wq_ref, wf_ref, wi_ref, wg_ref)

    def project(src_ref, row0, slot):
        u = src_ref[row0:row0 + HG_BLOCK, :]
        for k, w_ref in enumerate(w_refs):
            ps_ref[slot, k] = jnp.dot(u, w_ref[...], preferred_element_type=f32)

    @pl.when(pl.program_id(2) == 0)
    def _():
        st_ref[...] = jnp.zeros_like(st_ref)
        project(u_ref, 0, 0)

    lb = lb_ref[...]
    ng = ng_ref[...]
    row = lax.broadcasted_iota(jnp.int32, (rg, rg), 0)
    col = lax.broadcasted_iota(jnp.int32, (rg, rg), 1)
    causal = (row >= col) & ((row // c) == (col // c))
    tril = jnp.where(causal, 1.0, 0.0).astype(bf16)
    zero_blk = jnp.zeros((HEAD, HEAD), bf16)
    nt = (((1,), (1,)), ((), ()))
    tn = (((0,), (0,)), ((), ()))

    def decays(slot, g):
        rows = slice(g * rg, (g + 1) * rg)
        f = lb + (1.0 - lb) * _sigmoid(ps_ref[slot, 1, rows, :])
        b = functools.reduce(lambda x, y: x + y,
                             [jnp.dot(tril, part, preferred_element_type=f32) for part in _split3(jnp.log(f))])
        return slot, rows, 1.0 - f, b

    def operands(slot, rows, k, b):
        q = ps_ref[slot, 0, rows, :]
        qd, kd, q0, decay, us = [], [], [], [], []
        for ci in range(nck):
            sl = slice(ci * c, (ci + 1) * c)
            bc = b[sl, :]
            b_mid = bc[half - 1:half, :]
            b_last = bc[c - 1:c, :]
            qm = q[sl, :] * jnp.exp(bc - b_mid)
            qd.append(qm.astype(bf16))
            kd.append((k[sl, :] * jnp.exp(b_mid - bc)).astype(bf16))
            q0.append((qm * jnp.exp(b_mid)).astype(bf16))
            decay.append(jnp.exp(b_last))
            us.append(jnp.exp(b_last - b_mid))
        return slot, rows, qd, kd, q0, decay, us, ps_ref[slot, 2, rows, :].astype(bf16)

    def intra_chunk(slot, rows, qd, kd, q0, decay, us, v):
        qd_all = jnp.concatenate(qd, axis=0)
        kd_all = jnp.concatenate(kd, axis=0)
        intra = []
        for h in range(2):
            ln = slice(h * HEAD, (h + 1) * HEAD)
            s = lax.dot_general(qd_all[:, ln], kd_all[:, ln], nt, preferred_element_type=f32)
            s = jnp.where(causal, s, 0.0).astype(bf16)
            intra.append(jnp.dot(s, v[:, ln], preferred_element_type=f32))
        upd = [[lax.dot_general(v[ci * c:(ci + 1) * c, h * HEAD:(h + 1) * HEAD],
                                kd[ci][:, h * HEAD:(h + 1) * HEAD], tn, preferred_element_type=f32)
                * us[ci][:, h * HEAD:(h + 1) * HEAD] for h in range(2)] for ci in range(nck)]
        return slot, rows, jnp.concatenate(intra, axis=1), upd, q0, decay

    def recur(out_row0, st, slot, rows, o_intra, upd, q0, decay):
        gate = ps_ref[slot, 3, rows, :]
        for ci in range(nck):
            sl = slice(ci * c, (ci + 1) * c)
            s0 = st[0].astype(bf16)
            s1 = st[1].astype(bf16)
            both = jnp.concatenate([jnp.concatenate([s0, zero_blk], axis=1),
                                    jnp.concatenate([zero_blk, s1], axis=1)], axis=0)
            o = o_intra[sl, :] + lax.dot_general(q0[ci], both, nt, preferred_element_type=f32)
            for h in range(2):
                ln = slice(h * HEAD, (h + 1) * HEAD)
                st[h] = st[h] * decay[ci][:, ln] + upd[ci][h]
                oh = o[:, ln]
                oh = oh * lax.rsqrt(jnp.mean(oh * oh, axis=-1, keepdims=True) + EPS)
                gh = gate[sl, ln]
                r0 = out_row0 + rows.start + ci * c
                o_ref[r0:r0 + c, ln] = (oh * ng[:, ln] * (gh * _sigmoid(gh))).astype(o_ref.dtype)
        return st

    def mix(slot, out_row0, st):
        stage = [decays(slot, g) for g in range(HG_BLOCK // rg)]
        stage = [operands(*x) for x in stage]
        stage = [intra_chunk(*x) for x in stage]
        for x in stage:
            st = recur(out_row0, st, *x)
        return st

    st = [st_ref[0], st_ref[1]]
    project(u_ref, HG_BLOCK, 1)
    st = mix(0, 0, st)
    project(un_ref, 0, 0)
    st = mix(1, HG_BLOCK, st)
    st_ref[0] = st[0]
    st_ref[1] = st[1]


def _hgrn(u, w_in, lb, ng, batch, seq):
    n, dm = u.shape
    w = w_in.shape[1] // 4
    pairs = w // (2 * HEAD)
    nt = seq // HG_TILE
    blk = (HG_TILE, 2 * HEAD)

    def wspec(k):
        return pl.BlockSpec((dm, 2 * HEAD), lambda b, pr, t, k=k: (0, k * pairs + pr))

    vec = pl.BlockSpec((1, 2 * HEAD), lambda b, pr, t: (0, pr))
    return pl.pallas_call(
        _hgrn_kernel,
        out_shape=jax.ShapeDtypeStruct((n, w), bf16),
        grid=(batch, pairs, nt),
        in_specs=[pl.BlockSpec((HG_TILE, dm), lambda b, pr, t: (b * nt + t, 0)),
                  pl.BlockSpec((HG_TILE, dm), lambda b, pr, t: (b * nt + jnp.minimum(t + 1, nt - 1), 0)),
                  wspec(0), wspec(1), wspec(2), wspec(3), vec, vec],
        out_specs=pl.BlockSpec(blk, lambda b, pr, t: (b * nt + t, pr)),
        scratch_shapes=[pltpu.VMEM((2, HEAD, HEAD), f32),
                        pltpu.VMEM((2, 4, HG_BLOCK, 2 * HEAD), f32)],
        compiler_params=_cparams(("parallel", "parallel", "arbitrary")),
        name="hgrn2",
    )(u, u, w_in, w_in, w_in, w_in, lb.reshape(1, w), ng.reshape(1, w))


MERGE_D = 4
LOG2E = math.log2(math.e)


def _attn_kernel(q0_ref, k0_ref, v0_ref, gate_ref, q1_ref, k1_ref, v1_ref, q2_ref, k2_ref, v2_ref,
                 bias_ref, o_ref, ck0, cv0, ck1, cv1, ck2, cv2, oacc, lacc, macc, tmp, nat):
    t = pl.program_id(2)
    par = t % 2
    q_refs = (q0_ref, q1_ref, q2_ref)
    k_refs = (k0_ref, k1_ref, k2_ref)
    v_refs = (v0_ref, v1_ref, v2_ref)
    cks = (ck0, ck1, ck2)
    cvs = (cv0, cv1, cv2)
    scale2 = HEAD ** -0.5 * LOG2E
    blk = ATT_BLOCK
    res_rows = ATT_TILE // MERGE_D
    sub = blk // MERGE_D

    @pl.when(t == 0)
    def _():
        for g, (_, d) in enumerate(DILATED_GROUPS):
            cks[g][:, 2 * blk:, :] = jnp.zeros((d, blk, HEAD), bf16)
            cvs[g][:, 2 * blk:, :] = jnp.zeros((d, blk, HEAD), bf16)

    slot = pl.ds(pl.multiple_of(par * 2 * blk, blk), blk)
    for g, (_, d) in enumerate(DILATED_GROUPS):
        per_res = ATT_TILE // d
        for r in range(d):
            for src, dst in ((k_refs[g], cks[g]), (v_refs[g], cvs[g])):
                dst[r, blk:2 * blk, :] = src[r * per_res:r * per_res + blk, :]
                dst[r, slot, :] = src[(r + 1) * per_res - blk:(r + 1) * per_res, :]

    cwin = pl.ds(pl.multiple_of((1 - par) * blk, blk), 2 * blk)
    colid = lax.broadcasted_iota(jnp.int32, (blk, 2 * blk), 1)
    pen = jnp.where(colid >= blk, jnp.where(t == 0, NEG, 0.0).astype(f32), 0.0)

    def carry_bias(g):
        b = bias_ref[g]
        swapped = jnp.concatenate([b[:, blk:], b[:, :blk]], axis=1)
        return jnp.where(par == 1, b, swapped) + pen

    def attend(q, kwin, vwin, bias):
        s = lax.dot_general(q, kwin, (((1,), (1,)), ((), ())), preferred_element_type=f32)
        s = s * scale2 + bias
        m = jnp.max(s, axis=-1, keepdims=True)
        p = jnp.exp2(s - m)
        l = jnp.sum(p, axis=-1, keepdims=True)
        acc = jnp.dot(p.astype(bf16), vwin, preferred_element_type=f32)
        return acc, jnp.broadcast_to(l, (blk, HEAD)), jnp.broadcast_to(m, (blk, HEAD))

    def block(g, r, j):
        per_res = ATT_TILE // DILATED_GROUPS[g][1]
        q = q_refs[g][r * per_res + j * blk:r * per_res + (j + 1) * blk, :]
        if j == 0:
            return attend(q, cks[g][r, cwin, :], cvs[g][r, cwin, :], carry_bias(g))
        win = slice(r * per_res + (j - 1) * blk, r * per_res + (j + 1) * blk)
        return attend(q, k_refs[g][win, :], v_refs[g][win, :], bias_ref[g])

    dsts = (oacc, lacc, macc)

    for j in range(ATT_TILE // blk):
        for k, val in enumerate(block(0, 0, j)):
            tmp[3 * j + k] = val
            for r4 in range(MERGE_D):
                dsts[k][0, r4 * res_rows + j * sub:r4 * res_rows + (j + 1) * sub, :] = (
                    tmp[3 * j + k, pl.ds(r4, sub, stride=MERGE_D), :])

    d1 = DILATED_GROUPS[1][1]
    for r in range(d1):
        for j in range(ATT_TILE // d1 // blk):
            row0 = r * (ATT_TILE // d1) + j * blk
            for k, val in enumerate(block(1, r, j)):
                dsts[k][1, row0:row0 + blk, :] = val

    d2 = DILATED_GROUPS[2][1]
    for r16 in range(d2):
        rows = pl.ds((r16 % MERGE_D) * res_rows + r16 // MERGE_D, blk, stride=d2 // MERGE_D)
        for k, val in enumerate(block(2, r16, 0)):
            dsts[k][2, rows, :] = val

    def merge(c, carry):
        for r4 in range(MERGE_D):
            rows = pl.ds(pl.multiple_of(r4 * res_rows + c * blk, blk), blk)
            ms = [macc[g, rows, :] for g in range(N_GROUPS)]
            mx = functools.reduce(jnp.maximum, ms)
            ws = [jnp.exp2(x - mx) for x in ms]
            num = functools.reduce(lambda a, b: a + b, [w * oacc[g, rows, :] for g, w in enumerate(ws)])
            den = functools.reduce(lambda a, b: a + b, [w * lacc[g, rows, :] for g, w in enumerate(ws)])
            nat[pl.ds(c * blk * MERGE_D + r4, blk, stride=MERGE_D), :] = num / den
        return carry

    lax.fori_loop(0, res_rows // blk, merge, 0)

    rc = 256
    for c0 in range(0, ATT_TILE, rc):
        rows = slice(c0, c0 + rc)
        gate = gate_ref[rows, :].astype(f32)
        o_ref[rows, :] = (nat[rows, :] * (gate * _sigmoid(gate))).astype(o_ref.dtype)


def _attention(p, bias, batch, seq, heads):
    n = p.shape[0]
    nt = seq // ATT_TILE
    blk = (ATT_TILE, HEAD)

    def spec(k):
        return pl.BlockSpec(blk, lambda b, h, t, k=k: (b * nt + t, k * heads + h))

    scratch = []
    for _, d in DILATED_GROUPS:
        scratch += [pltpu.VMEM((d, 3 * ATT_BLOCK, HEAD), bf16)] * 2
    scratch += [pltpu.VMEM((N_GROUPS, ATT_TILE, HEAD), f32)] * 3
    scratch += [pltpu.VMEM((3 * ATT_TILE // ATT_BLOCK, ATT_BLOCK, HEAD), f32), pltpu.VMEM((ATT_TILE, HEAD), f32)]
    return pl.pallas_call(
        _attn_kernel,
        out_shape=jax.ShapeDtypeStruct((n, heads * HEAD), bf16),
        grid=(batch, heads, nt),
        in_specs=[spec(0), spec(1), spec(2), spec(3 * N_GROUPS), spec(3), spec(4), spec(5), spec(6), spec(7), spec(8),
                  pl.BlockSpec((N_GROUPS, None, ATT_BLOCK, 2 * ATT_BLOCK), lambda b, h, t: (0, h, 0, 0))],
        out_specs=pl.BlockSpec(blk, lambda b, h, t: (b * nt + t, h)),
        scratch_shapes=scratch,
        compiler_params=_cparams(("parallel", "parallel", "arbitrary")),
        name="dilated_attention",
    )(*([p] * 10), bias)


def _t5_bucket(dist):
    max_exact = N_BUCKETS // 2
    df = jnp.maximum(dist, 1).astype(f32)
    large = max_exact + (jnp.log(df / max_exact) / math.log(MAX_DISTANCE / max_exact)
                         * (N_BUCKETS - max_exact)).astype(jnp.int32)
    large = jnp.minimum(large, N_BUCKETS - 1)
    return jnp.where(dist < max_exact, dist, large)


def _bias_tables(rel_bias, heads):
    a = jnp.arange(ATT_BLOCK)[:, None]
    c = jnp.arange(2 * ATT_BLOCK)[None, :]
    rel = ATT_BLOCK + a - c
    tables = []
    for g, (window, d) in enumerate(DILATED_GROUPS):
        assert window // d == ATT_BLOCK and ATT_TILE % (d * ATT_BLOCK) == 0
        band = (rel >= 0) & (rel <= window // d)
        onehot = jax.nn.one_hot(_t5_bucket(jnp.maximum(rel, 0) * d), N_BUCKETS, dtype=f32)
        tab = jnp.einsum("acb,bh->hac", onehot, rel_bias[:, g * heads:(g + 1) * heads].astype(f32),
                         precision=lax.Precision.HIGHEST)
        tables.append(jnp.where(band[None], tab * LOG2E, NEG))
    return jnp.stack(tables, axis=0)


PERM_ROWS = 256


def _groupproj_kernel(a_ref, w_ref, o_ref, perm_ref, *, col_blocks_per_group):
    j = pl.program_id(1)
    tm = a_ref.shape[0]

    @pl.when(j == 0)
    def _():
        i = lax.broadcasted_iota(jnp.int32, (PERM_ROWS, PERM_ROWS), 0)
        k = lax.broadcasted_iota(jnp.int32, (PERM_ROWS, PERM_ROWS), 1)
        for g, (_, d) in enumerate(DILATED_GROUPS[1:]):
            per = PERM_ROWS // d
            perm = jnp.where(k == (i % per) * d + i // per, 1.0, 0.0).astype(bf16)
            for sb in range(tm // PERM_ROWS):
                y = jnp.dot(perm, a_ref[sb * PERM_ROWS:(sb + 1) * PERM_ROWS, :],
                            preferred_element_type=f32).astype(bf16)
                for r in range(d):
                    dst = r * (tm // d) + sb * per
                    perm_ref[g, dst:dst + per, :] = y[r * per:(r + 1) * per, :]

    def project(lhs_ref):
        rc = 512
        for r in range(tm // rc):
            rows = slice(r * rc, (r + 1) * rc)
            o_ref[rows, :] = jnp.dot(lhs_ref[rows, :], w_ref[...], preferred_element_type=f32).astype(o_ref.dtype)

    grp = j // col_blocks_per_group
    for g in range(1, N_GROUPS):
        pl.when(grp == g)(functools.partial(project, perm_ref.at[g - 1]))
    pl.when((grp == 0) | (grp >= N_GROUPS))(functools.partial(project, a_ref))


def _groupproj(u, w, width, tn=1024):
    n, k = u.shape
    c = w.shape[1]
    tn = math.gcd(tn, width)
    return pl.pallas_call(
        functools.partial(_groupproj_kernel, col_blocks_per_group=3 * width // tn),
        out_shape=jax.ShapeDtypeStruct((n, c), bf16),
        grid=(n // ATT_TILE, c // tn),
        in_specs=[pl.BlockSpec((ATT_TILE, k), lambda i, j: (i, 0)),
                  pl.BlockSpec((k, tn), lambda i, j: (0, j))],
        out_specs=pl.BlockSpec((ATT_TILE, tn), lambda i, j: (i, j)),
        scratch_shapes=[pltpu.VMEM((N_GROUPS - 1, ATT_TILE, k), bf16)],
        compiler_params=_cparams(("parallel", "arbitrary")),
        name="group_proj",
    )(u, w)


def kernel(x, ln_g, hg_w_in, hg_lb_logits, hg_norm_g, hg_w_out, att_w_in, att_w_out, rel_bias, final_g):
    batch, seq, d_model = x.shape
    n = batch * seq
    w = hg_w_out.shape[1]
    heads = w // HEAD
    assert seq % ATT_TILE == 0 and seq % HG_TILE == 0 and w % (2 * HEAD) == 0

    lower = jnp.cumsum(jax.nn.softmax(hg_lb_logits.astype(f32), axis=0), axis=0)
    h0 = x.reshape(n, d_model)

    u0 = _rmsnorm(h0, ln_g[0], bf16)
    y0 = _hgrn(u0, hg_w_in[0].astype(bf16), lower[0], hg_norm_g[0], batch, seq)
    h1, u1 = _outproj(y0, hg_w_out[0].astype(bf16), h0, ln_g[1], bf16, emit_h=True)

    p1 = _groupproj(u1, att_w_in[0].astype(bf16), w)
    y1 = _attention(p1, _bias_tables(rel_bias, heads), batch, seq, heads)
    (out,) = _outproj(y1, att_w_out[0].astype(bf16), h1, final_g, f32, emit_h=False)
    return out.reshape(batch, seq, d_model)
```

```python
import functools
import math

import jax
import jax.numpy as jnp
from jax import lax
from jax.experimental import pallas as pl
from jax.experimental.pallas import tpu as pltpu

EPS = 1e-6
HEAD = 128
HG_CHUNK = 64
ATT_BLOCK = 128
DILATED_GROUPS = ((128, 1), (512, 4), (2048, 16))
N_GROUPS = len(DILATED_GROUPS)
N_BUCKETS = 32
MAX_DISTANCE = 2048
ATT_TILE = ATT_BLOCK * max(d for _, d in DILATED_GROUPS)
NEG = -1e30

VMEM_LIMIT = 56 * 1024 * 1024

f32 = jnp.float32
bf16 = jnp.bfloat16


def _cparams(sem):
    return pltpu.CompilerParams(dimension_semantics=sem, vmem_limit_bytes=VMEM_LIMIT)


def _rmsnorm_kernel(x_ref, g_ref, o_ref):
    x = x_ref[...]
    ms = jnp.mean(x * x, axis=-1, keepdims=True)
    o_ref[...] = (x * lax.rsqrt(ms + EPS) * g_ref[...]).astype(o_ref.dtype)


def _rmsnorm(x, g, out_dtype, tm=1024):
    n, d = x.shape
    return pl.pallas_call(
        _rmsnorm_kernel,
        out_shape=jax.ShapeDtypeStruct((n, d), out_dtype),
        grid=(n // tm,),
        in_specs=[pl.BlockSpec((tm, d), lambda i: (i, 0)),
                  pl.BlockSpec((1, d), lambda i: (0, 0))],
        out_specs=pl.BlockSpec((tm, d), lambda i: (i, 0)),
        compiler_params=_cparams(("parallel",)),
        name="rmsnorm",
    )(x, g.reshape(1, d))


def _matmul_kernel(a_ref, w_ref, o_ref, *, row_chunk):
    tm = a_ref.shape[0]
    for r in range(tm // row_chunk):
        rows = slice(r * row_chunk, (r + 1) * row_chunk)
        o_ref[rows, :] = jnp.dot(a_ref[rows, :], w_ref[...], preferred_element_type=f32).astype(o_ref.dtype)


def _matmul(a, w, out_dtype, tm, tn):
    n, k = a.shape
    c = w.shape[1]
    tm = min(tm, n)
    tn = math.gcd(tn, c)
    return pl.pallas_call(
        functools.partial(_matmul_kernel, row_chunk=min(512, tm)),
        out_shape=jax.ShapeDtypeStruct((n, c), out_dtype),
        grid=(n // tm, c // tn),
        in_specs=[pl.BlockSpec((tm, k), lambda i, j: (i, 0)),
                  pl.BlockSpec((k, tn), lambda i, j: (0, j))],
        out_specs=pl.BlockSpec((tm, tn), lambda i, j: (i, j)),
        compiler_params=_cparams(("parallel", "arbitrary")),
        name="proj_matmul",
    )(a, w)


def _outproj_kernel(y_ref, w_ref, h_ref, g_ref, *out_refs, emit_h):
    h = h_ref[...] + jnp.dot(y_ref[...], w_ref[...], preferred_element_type=f32)
    if emit_h:
        out_refs[0][...] = h
    n_ref = out_refs[-1]
    ms = jnp.mean(h * h, axis=-1, keepdims=True)
    n_ref[...] = (h * lax.rsqrt(ms + EPS) * g_ref[...]).astype(n_ref.dtype)


def _outproj(y, w, h, g, norm_dtype, emit_h, tm=512):
    n, k = y.shape
    d = w.shape[1]
    tm = min(tm, n)
    row = lambda i: (i, 0)
    out_shape = [jax.ShapeDtypeStruct((n, d), norm_dtype)]
    out_specs = [pl.BlockSpec((tm, d), row)]
    if emit_h:
        out_shape.insert(0, jax.ShapeDtypeStruct((n, d), f32))
        out_specs.insert(0, pl.BlockSpec((tm, d), row))
    return pl.pallas_call(
        functools.partial(_outproj_kernel, emit_h=emit_h),
        out_shape=out_shape,
        grid=(n // tm,),
        in_specs=[pl.BlockSpec((tm, k), row),
                  pl.BlockSpec((k, d), lambda i: (0, 0)),
                  pl.BlockSpec((tm, d), row),
                  pl.BlockSpec((1, d), lambda i: (0, 0))],
        out_specs=out_specs,
        compiler_params=_cparams(("parallel",)),
        name="outproj",
    )(y, w, h, g.reshape(1, d))


HG_GROUP = 4 * HG_CHUNK
HG_BLOCK = 2 * HG_GROUP
HG_TILE = 2 * HG_BLOCK


def _sigmoid(x):
    return 1.0 / (1.0 + jnp.exp(-x))


def _split3(x):
    hi = x.astype(bf16)
    r1 = x - hi.astype(f32)
    mid = r1.astype(bf16)
    lo = (r1 - mid.astype(f32)).astype(bf16)
    return hi, mid, lo


def _hgrn_kernel(u_ref, un_ref, wq_ref, wf_ref, wi_ref, wg_ref, lb_ref, ng_ref, o_ref, st_ref, ps_ref):
    c = HG_CHUNK
    half = c // 2
    rg = HG_GROUP
    nck = rg // c
    w_refs = (wq_ref, wf_ref, wi_ref, wg_ref)

    def project(src_ref, row0, slot):
        u = src_ref[row0:row0 + HG_BLOCK, :]
        for k, w_ref in enumerate(w_refs):
            ps_ref[slot, k] = jnp.dot(u, w_ref[...], preferred_element_type=f32)

    @pl.when(pl.program_id(2) == 0)
    def _():
        st_ref[...] = jnp.zeros_like(st_ref)
        project(u_ref, 0, 0)

    lb = lb_ref[...]
    ng = ng_ref[...]
    row = lax.broadcasted_iota(jnp.int32, (rg, rg), 0)
    col = lax.broadcasted_iota(jnp.int32, (rg, rg), 1)
    causal = (row >= col) & ((row // c) == (col // c))
    tril = jnp.where(causal, 1.0, 0.0).astype(bf16)
    zero_blk = jnp.zeros((HEAD, HEAD), bf16)
    nt = (((1,), (1,)), ((), ()))
    tn = (((0,), (0,)), ((), ()))

    def decays(slot, g):
        rows = slice(g * rg, (g + 1) * rg)
        f = lb + (1.0 - lb) * _sigmoid(ps_ref[slot, 1, rows, :])
        b = functools.reduce(lambda x, y: x + y,
                             [jnp.dot(tril, part, preferred_element_type=f32) for part in _split3(jnp.log(f))])
        return slot, rows, 1.0 - f, b

    def operands(slot, rows, k, b):
        q = ps_ref[slot, 0, rows, :]
        qd, kd, q0, decay, us = [], [], [], [], []
        for ci in range(nck):
            sl = slice(ci * c, (ci + 1) * c)
            bc = b[sl, :]
            b_mid = bc[half - 1:half, :]
            b_last = bc[c - 1:c, :]
            qm = q[sl, :] * jnp.exp(bc - b_mid)
            qd.append(qm.astype(bf16))
            kd.append((k[sl, :] * jnp.exp(b_mid - bc)).astype(bf16))
            q0.append((qm * jnp.exp(b_mid)).astype(bf16))
            decay.append(jnp.exp(b_last))
            us.append(jnp.exp(b_last - b_mid))
        return slot, rows, qd, kd, q0, decay, us, ps_ref[slot, 2, rows, :].astype(bf16)

    def intra_chunk(slot, rows, qd, kd, q0, decay, us, v):
        qd_all = jnp.concatenate(qd, axis=0)
        kd_all = jnp.concatenate(kd, axis=0)
        intra = []
        for h in range(2):
            ln = slice(h * HEAD, (h + 1) * HEAD)
            s = lax.dot_general(qd_all[:, ln], kd_all[:, ln], nt, preferred_element_type=f32)
            s = jnp.where(causal, s, 0.0).astype(bf16)
            intra.append(jnp.dot(s, v[:, ln], preferred_element_type=f32))
        upd = [[lax.dot_general(v[ci * c:(ci + 1) * c, h * HEAD:(h + 1) * HEAD],
                                kd[ci][:, h * HEAD:(h + 1) * HEAD], tn, preferred_element_type=f32)
                * us[ci][:, h * HEAD:(h + 1) * HEAD] for h in range(2)] for ci in range(nck)]
        return slot, rows, jnp.concatenate(intra, axis=1), upd, q0, decay

    def recur(out_row0, st, slot, rows, o_intra, upd, q0, decay):
        gate = ps_ref[slot, 3, rows, :]
        for ci in range(nck):
            sl = slice(ci * c, (ci + 1) * c)
            s0 = st[0].astype(bf16)
            s1 = st[1].astype(bf16)
            both = jnp.concatenate([jnp.concatenate([s0, zero_blk], axis=1),
                                    jnp.concatenate([zero_blk, s1], axis=1)], axis=0)
            o = o_intra[sl, :] + lax.dot_general(q0[ci], both, nt, preferred_element_type=f32)
            for h in range(2):
                ln = slice(h * HEAD, (h + 1) * HEAD)
                st[h] = st[h] * decay[ci][:, ln] + upd[ci][h]
                oh = o[:, ln]
                oh = oh * lax.rsqrt(jnp.mean(oh * oh, axis=-1, keepdims=True) + EPS)
                gh = gate[sl, ln]
                r0 = out_row0 + rows.start + ci * c
                o_ref[r0:r0 + c, ln] = (oh * ng[:, ln] * (gh * _sigmoid(gh))).astype(o_ref.dtype)
        return st

    def mix(slot, out_row0, st):
        stage = [decays(slot, g) for g in range(HG_BLOCK // rg)]
        stage = [operands(*x) for x in stage]
        stage = [intra_chunk(*x) for x in stage]
        for x in stage:
            st = recur(out_row0, st, *x)
        return st

    st = [st_ref[0], st_ref[1]]
    project(u_ref, HG_BLOCK, 1)
    st = mix(0, 0, st)
    project(un_ref, 0, 0)
    st = mix(1, HG_BLOCK, st)
    st_ref[0] = st[0]
    st_ref[1] = st[1]


def _hgrn(u, w_in, lb, ng, batch, seq):
    n, dm = u.shape
    w = w_in.shape[1] // 4
    pairs = w // (2 * HEAD)
    nt = seq // HG_TILE
    blk = (HG_TILE, 2 * HEAD)

    def wspec(k):
        return pl.BlockSpec((dm, 2 * HEAD), lambda b, pr, t, k=k: (0, k * pairs + pr))

    vec = pl.BlockSpec((1, 2 * HEAD), lambda b, pr, t: (0, pr))
    return pl.pallas_call(
        _hgrn_kernel,
        out_shape=jax.ShapeDtypeStruct((n, w), bf16),
        grid=(batch, pairs, nt),
        in_specs=[pl.BlockSpec((HG_TILE, dm), lambda b, pr, t: (b * nt + t, 0)),
                  pl.BlockSpec((HG_TILE, dm), lambda b, pr, t: (b * nt + jnp.minimum(t + 1, nt - 1), 0)),
                  wspec(0), wspec(1), wspec(2), wspec(3), vec, vec],
        out_specs=pl.BlockSpec(blk, lambda b, pr, t: (b * nt + t, pr)),
        scratch_shapes=[pltpu.VMEM((2, HEAD, HEAD), f32),
                        pltpu.VMEM((2, 4, HG_BLOCK, 2 * HEAD), f32)],
        compiler_params=_cparams(("parallel", "parallel", "arbitrary")),
        name="hgrn2",
    )(u, u, w_in, w_in, w_in, w_in, lb.reshape(1, w), ng.reshape(1, w))


MERGE_D = 4
LOG2E = math.log2(math.e)


def _attn_kernel(u_ref, wq_ref, wk_ref, wv_ref, wg_ref, q1_ref, k1_ref, v1_ref, q2_ref, k2_ref, v2_ref,
                 bias_ref, o_ref, ck0, cv0, ck1, cv1, ck2, cv2, oacc, lacc, macc, tmp, nat, p0):
    t = pl.program_id(2)
    par = t % 2
    q_refs = (p0.at[0], q1_ref, q2_ref)
    k_refs = (p0.at[1], k1_ref, k2_ref)
    v_refs = (p0.at[2], v1_ref, v2_ref)
    gate_ref = p0.at[3]
    cks = (ck0, ck1, ck2)
    cvs = (cv0, cv1, cv2)
    scale2 = HEAD ** -0.5 * LOG2E
    blk = ATT_BLOCK
    res_rows = ATT_TILE // MERGE_D
    sub = blk // MERGE_D

    @pl.when(t == 0)
    def _():
        for g, (_, d) in enumerate(DILATED_GROUPS):
            cks[g][:, 2 * blk:, :] = jnp.zeros((d, blk, HEAD), bf16)
            cvs[g][:, 2 * blk:, :] = jnp.zeros((d, blk, HEAD), bf16)

    proj_rows = 512

    def project(c0):
        w0 = jnp.concatenate([wq_ref[...], wk_ref[...], wv_ref[...], wg_ref[...]], axis=1)
        y = jnp.dot(u_ref[c0:c0 + proj_rows, :], w0, preferred_element_type=f32).astype(bf16)
        for k in range(4):
            p0[k, c0:c0 + proj_rows, :] = y[:, k * HEAD:(k + 1) * HEAD]

    slot = pl.ds(pl.multiple_of(par * 2 * blk, blk), blk)

    def stage_carry(g):
        d = DILATED_GROUPS[g][1]
        per_res = ATT_TILE // d
        for r in range(d):
            for src, dst in ((k_refs[g], cks[g]), (v_refs[g], cvs[g])):
                dst[r, blk:2 * blk, :] = src[r * per_res:r * per_res + blk, :]
                dst[r, slot, :] = src[(r + 1) * per_res - blk:(r + 1) * per_res, :]

    cwin = pl.ds(pl.multiple_of((1 - par) * blk, blk), 2 * blk)
    colid = lax.broadcasted_iota(jnp.int32, (blk, 2 * blk), 1)
    pen = jnp.where(colid >= blk, jnp.where(t == 0, NEG, 0.0).astype(f32), 0.0)

    def carry_bias(g):
        b = bias_ref[g]
        swapped = jnp.concatenate([b[:, blk:], b[:, :blk]], axis=1)
        return jnp.where(par == 1, b, swapped) + pen

    def attend(q, kwin, vwin, bias):
        s = lax.dot_general(q, kwin, (((1,), (1,)), ((), ())), preferred_element_type=f32)
        s = s * scale2 + bias
        m = jnp.max(s, axis=-1, keepdims=True)
        p = jnp.exp2(s - m)
        l = jnp.sum(p, axis=-1, keepdims=True)
        acc = jnp.dot(p.astype(bf16), vwin, preferred_element_type=f32)
        return acc, jnp.broadcast_to(l, (blk, HEAD)), jnp.broadcast_to(m, (blk, HEAD))

    def block(g, r, j):
        per_res = ATT_TILE // DILATED_GROUPS[g][1]
        q = q_refs[g][r * per_res + j * blk:r * per_res + (j + 1) * blk, :]
        if j == 0:
            return attend(q, cks[g][r, cwin, :], cvs[g][r, cwin, :], carry_bias(g))
        win = slice(r * per_res + (j - 1) * blk, r * per_res + (j + 1) * blk)
        return attend(q, k_refs[g][win, :], v_refs[g][win, :], bias_ref[g])

    dsts = (oacc, lacc, macc)

    def group1(r, j):
        row0 = r * (ATT_TILE // d1) + j * blk
        for k, val in enumerate(block(1, r, j)):
            dsts[k][1, row0:row0 + blk, :] = val

    def group2(r16):
        rows = pl.ds((r16 % MERGE_D) * res_rows + r16 // MERGE_D, blk, stride=d2 // MERGE_D)
        for k, val in enumerate(block(2, r16, 0)):
            dsts[k][2, rows, :] = val

    def group0(j):
        for k, val in enumerate(block(0, 0, j)):
            tmp[3 * j + k] = val
            for r4 in range(MERGE_D):
                dsts[k][0, r4 * res_rows + j * sub:r4 * res_rows + (j + 1) * sub, :] = (
                    tmp[3 * j + k, pl.ds(r4, sub, stride=MERGE_D), :])

    d1 = DILATED_GROUPS[1][1]
    d2 = DILATED_GROUPS[2][1]
    stage_carry(1)
    stage_carry(2)
    dilated = ([functools.partial(group1, r, j) for r in range(d1) for j in range(ATT_TILE // d1 // blk)]
               + [functools.partial(group2, r16) for r16 in range(d2)])
    n_proj = ATT_TILE // proj_rows
    per_proj = len(dilated) // n_proj
    for i in range(n_proj):
        project(i * proj_rows)
        for task in dilated[i * per_proj:(i + 1) * per_proj]:
            task()
    stage_carry(0)
    for j in range(ATT_TILE // blk):
        group0(j)

    def merge(c, carry):
        for r4 in range(MERGE_D):
            rows = pl.ds(pl.multiple_of(r4 * res_rows + c * blk, blk), blk)
            ms = [macc[g, rows, :] for g in range(N_GROUPS)]
            mx = functools.reduce(jnp.maximum, ms)
            ws = [jnp.exp2(x - mx) for x in ms]
            num = functools.reduce(lambda a, b: a + b, [w * oacc[g, rows, :] for g, w in enumerate(ws)])
            den = functools.reduce(lambda a, b: a + b, [w * lacc[g, rows, :] for g, w in enumerate(ws)])
            nat[pl.ds(c * blk * MERGE_D + r4, blk, stride=MERGE_D), :] = num / den
        return carry

    lax.fori_loop(0, res_rows // blk, merge, 0)

    rc = 256
    for c0 in range(0, ATT_TILE, rc):
        rows = slice(c0, c0 + rc)
        gate = gate_ref[rows, :].astype(f32)
        o_ref[rows, :] = (nat[rows, :] * (gate * _sigmoid(gate))).astype(o_ref.dtype)


def _attention(u, w_in, p, bias, batch, seq, heads):
    n, dm = u.shape
    nt = seq // ATT_TILE
    blk = (ATT_TILE, HEAD)

    def spec(k):
        return pl.BlockSpec(blk, lambda b, h, t, k=k: (b * nt + t, k * heads + h))

    def wspec(k):
        return pl.BlockSpec((dm, HEAD), lambda b, h, t, k=k: (0, k * heads + h))

    scratch = []
    for _, d in DILATED_GROUPS:
        scratch += [pltpu.VMEM((d, 3 * ATT_BLOCK, HEAD), bf16)] * 2
    scratch += [pltpu.VMEM((N_GROUPS, ATT_TILE, HEAD), f32)] * 3
    scratch += [pltpu.VMEM((3 * ATT_TILE // ATT_BLOCK, ATT_BLOCK, HEAD), f32), pltpu.VMEM((ATT_TILE, HEAD), f32)]
    scratch += [pltpu.VMEM((4, ATT_TILE, HEAD), bf16)]
    return pl.pallas_call(
        _attn_kernel,
        out_shape=jax.ShapeDtypeStruct((n, heads * HEAD), bf16),
        grid=(batch, heads, nt),
        in_specs=[pl.BlockSpec((ATT_TILE, dm), lambda b, h, t: (b * nt + t, 0)),
                  wspec(0), wspec(1), wspec(2), wspec(3 * N_GROUPS)] + [spec(k) for k in range(6)] + [
                  pl.BlockSpec((N_GROUPS, None, ATT_BLOCK, 2 * ATT_BLOCK), lambda b, h, t: (0, h, 0, 0))],
        out_specs=pl.BlockSpec(blk, lambda b, h, t: (b * nt + t, h)),
        scratch_shapes=scratch,
        compiler_params=_cparams(("parallel", "parallel", "arbitrary")),
        name="dilated_attention",
    )(u, w_in, w_in, w_in, w_in, *([p] * 6), bias)


def _t5_bucket(dist):
    max_exact = N_BUCKETS // 2
    df = jnp.maximum(dist, 1).astype(f32)
    large = max_exact + (jnp.log(df / max_exact) / math.log(MAX_DISTANCE / max_exact)
                         * (N_BUCKETS - max_exact)).astype(jnp.int32)
    large = jnp.minimum(large, N_BUCKETS - 1)
    return jnp.where(dist < max_exact, dist, large)


def _bias_tables(rel_bias, heads):
    a = jnp.arange(ATT_BLOCK)[:, None]
    c = jnp.arange(2 * ATT_BLOCK)[None, :]
    rel = ATT_BLOCK + a - c
    tables = []
    for g, (window, d) in enumerate(DILATED_GROUPS):
        assert window // d == ATT_BLOCK and ATT_TILE % (d * ATT_BLOCK) == 0
        band = (rel >= 0) & (rel <= window // d)
        onehot = jax.nn.one_hot(_t5_bucket(jnp.maximum(rel, 0) * d), N_BUCKETS, dtype=f32)
        tab = jnp.einsum("acb,bh->hac", onehot, rel_bias[:, g * heads:(g + 1) * heads].astype(f32),
                         precision=lax.Precision.HIGHEST)
        tables.append(jnp.where(band[None], tab * LOG2E, NEG))
    return jnp.stack(tables, axis=0)


PERM_ROWS = 256


def _groupproj_kernel(a_ref, w_ref, o_ref, perm_ref, *, col_blocks_per_group):
    j = pl.program_id(1)
    tm = a_ref.shape[0]

    @pl.when(j == 0)
    def _():
        i = lax.broadcasted_iota(jnp.int32, (PERM_ROWS, PERM_ROWS), 0)
        k = lax.broadcasted_iota(jnp.int32, (PERM_ROWS, PERM_ROWS), 1)
        for g, (_, d) in enumerate(DILATED_GROUPS[1:]):
            per = PERM_ROWS // d
            perm = jnp.where(k == (i % per) * d + i // per, 1.0, 0.0).astype(bf16)
            for sb in range(tm // PERM_ROWS):
                y = jnp.dot(perm, a_ref[sb * PERM_ROWS:(sb + 1) * PERM_ROWS, :],
                            preferred_element_type=f32).astype(bf16)
                for r in range(d):
                    dst = r * (tm // d) + sb * per
                    perm_ref[g, dst:dst + per, :] = y[r * per:(r + 1) * per, :]

    def project(lhs_ref):
        rc = 512
        for r in range(tm // rc):
            rows = slice(r * rc, (r + 1) * rc)
            o_ref[rows, :] = jnp.dot(lhs_ref[rows, :], w_ref[...], preferred_element_type=f32).astype(o_ref.dtype)

    grp = j // col_blocks_per_group
    for g in range(N_GROUPS - 1):
        pl.when(grp == g)(functools.partial(project, perm_ref.at[g]))


def _groupproj(u, w, width, tn=1024):
    n, k = u.shape
    c = 3 * (N_GROUPS - 1) * width
    tn = math.gcd(tn, width)
    first = 3 * width // tn
    return pl.pallas_call(
        functools.partial(_groupproj_kernel, col_blocks_per_group=3 * width // tn),
        out_shape=jax.ShapeDtypeStruct((n, c), bf16),
        grid=(n // ATT_TILE, c // tn),
        in_specs=[pl.BlockSpec((ATT_TILE, k), lambda i, j: (i, 0)),
                  pl.BlockSpec((k, tn), lambda i, j: (0, j + first))],
        out_specs=pl.BlockSpec((ATT_TILE, tn), lambda i, j: (i, j)),
        scratch_shapes=[pltpu.VMEM((N_GROUPS - 1, ATT_TILE, k), bf16)],
        compiler_params=_cparams(("parallel", "arbitrary")),
        name="group_proj",
    )(u, w)


def kernel(x, ln_g, hg_w_in, hg_lb_logits, hg_norm_g, hg_w_out, att_w_in, att_w_out, rel_bias, final_g):
    batch, seq, d_model = x.shape
    n = batch * seq
    w = hg_w_out.shape[1]
    heads = w // HEAD
    assert seq % ATT_TILE == 0 and seq % HG_TILE == 0 and w % (2 * HEAD) == 0

    lower = jnp.cumsum(jax.nn.softmax(hg_lb_logits.astype(f32), axis=0), axis=0)
    h0 = x.reshape(n, d_model)

    u0 = _rmsnorm(h0, ln_g[0], bf16)
    y0 = _hgrn(u0, hg_w_in[0].astype(bf16), lower[0], hg_norm_g[0], batch, seq)
    h1, u1 = _outproj(y0, hg_w_out[0].astype(bf16), h0, ln_g[1], bf16, emit_h=True)

    wa = att_w_in[0].astype(bf16)
    y1 = _attention(u1, wa, _groupproj(u1, wa, w), _bias_tables(rel_bias, heads), batch, seq, heads)
    (out,) = _outproj(y1, att_w_out[0].astype(bf16), h1, final_g, f32, emit_h=False)
    return out.reshape(batch, seq, d_model)
```

```python
import functools
import math

import jax
import jax.numpy as jnp
from jax import lax
from jax.experimental import pallas as pl
from jax.experimental.pallas import tpu as pltpu

EPS = 1e-6
HEAD = 128
HG_CHUNK = 64
ATT_BLOCK = 128
DILATED_GROUPS = ((128, 1), (512, 4), (2048, 16))
N_GROUPS = len(DILATED_GROUPS)
N_BUCKETS = 32
MAX_DISTANCE = 2048
ATT_TILE = ATT_BLOCK * max(d for _, d in DILATED_GROUPS)
NEG = -1e30

VMEM_LIMIT = 56 * 1024 * 1024

f32 = jnp.float32
bf16 = jnp.bfloat16


def _cparams(sem):
    return pltpu.CompilerParams(dimension_semantics=sem, vmem_limit_bytes=VMEM_LIMIT)


def _rmsnorm_kernel(x_ref, g_ref, o_ref):
    x = x_ref[...]
    ms = jnp.mean(x * x, axis=-1, keepdims=True)
    o_ref[...] = (x * lax.rsqrt(ms + EPS) * g_ref[...]).astype(o_ref.dtype)


def _rmsnorm(x, g, out_dtype, tm=1024):
    n, d = x.shape
    return pl.pallas_call(
        _rmsnorm_kernel,
        out_shape=jax.ShapeDtypeStruct((n, d), out_dtype),
        grid=(n // tm,),
        in_specs=[pl.BlockSpec((tm, d), lambda i: (i, 0)),
                  pl.BlockSpec((1, d), lambda i: (0, 0))],
        out_specs=pl.BlockSpec((tm, d), lambda i: (i, 0)),
        compiler_params=_cparams(("parallel",)),
        name="rmsnorm",
    )(x, g.reshape(1, d))


def _matmul_kernel(a_ref, w_ref, o_ref, *, row_chunk):
    tm = a_ref.shape[0]
    for r in range(tm // row_chunk):
        rows = slice(r * row_chunk, (r + 1) * row_chunk)
        o_ref[rows, :] = jnp.dot(a_ref[rows, :], w_ref[...], preferred_element_type=f32).astype(o_ref.dtype)


def _matmul(a, w, out_dtype, tm, tn):
    n, k = a.shape
    c = w.shape[1]
    tm = min(tm, n)
    tn = math.gcd(tn, c)
    return pl.pallas_call(
        functools.partial(_matmul_kernel, row_chunk=min(512, tm)),
        out_shape=jax.ShapeDtypeStruct((n, c), out_dtype),
        grid=(n // tm, c // tn),
        in_specs=[pl.BlockSpec((tm, k), lambda i, j: (i, 0)),
                  pl.BlockSpec((k, tn), lambda i, j: (0, j))],
        out_specs=pl.BlockSpec((tm, tn), lambda i, j: (i, j)),
        compiler_params=_cparams(("parallel", "arbitrary")),
        name="proj_matmul",
    )(a, w)


def _outproj_kernel(y_ref, w_ref, h_ref, g_ref, *refs, emit_h):
    *out_refs, wb_ref = refs

    @pl.when(pl.program_id(0) == 0)
    def _():
        wb_ref[...] = w_ref[...].astype(bf16)

    h = h_ref[...] + jnp.dot(y_ref[...], wb_ref[...], preferred_element_type=f32)
    if emit_h:
        out_refs[0][...] = h
    n_ref = out_refs[-1]
    ms = jnp.mean(h * h, axis=-1, keepdims=True)
    n_ref[...] = (h * lax.rsqrt(ms + EPS) * g_ref[...]).astype(n_ref.dtype)


def _outproj(y, w, h, g, norm_dtype, emit_h, tm=1024):
    n, k = y.shape
    d = w.shape[1]
    tm = min(tm, n)
    row = lambda i: (i, 0)
    out_shape = [jax.ShapeDtypeStruct((n, d), norm_dtype)]
    out_specs = [pl.BlockSpec((tm, d), row)]
    if emit_h:
        out_shape.insert(0, jax.ShapeDtypeStruct((n, d), f32))
        out_specs.insert(0, pl.BlockSpec((tm, d), row))
    return pl.pallas_call(
        functools.partial(_outproj_kernel, emit_h=emit_h),
        out_shape=out_shape,
        grid=(n // tm,),
        in_specs=[pl.BlockSpec((tm, k), row),
                  pl.BlockSpec((k, d), lambda i: (0, 0)),
                  pl.BlockSpec((tm, d), row),
                  pl.BlockSpec((1, d), lambda i: (0, 0))],
        out_specs=out_specs,
        scratch_shapes=[pltpu.VMEM((k, d), bf16)],
        compiler_params=_cparams(("arbitrary",)),
        name="outproj",
    )(y, w, h, g.reshape(1, d))


HG_GROUP = 4 * HG_CHUNK
HG_BLOCK = 2 * HG_GROUP
HG_TILE = 2 * HG_BLOCK


def _sigmoid(x):
    return 1.0 / (1.0 + jnp.exp(-x))


def _split2(x):
    hi = x.astype(bf16)
    return hi, (x - hi.astype(f32)).astype(bf16)


def _hgrn_kernel(u_ref, un_ref, wq_ref, wf_ref, wi_ref, wg_ref, lb_ref, ng_ref, o_ref, st_ref, ps_ref, wb_ref):
    c = HG_CHUNK
    half = c // 2
    rg = HG_GROUP
    nck = rg // c
    w_refs = (wq_ref, wf_ref, wi_ref, wg_ref)

    def project(src_ref, row0, slot):
        u = src_ref[row0:row0 + HG_BLOCK, :]
        for k in range(len(w_refs)):
            ps_ref[slot, k] = jnp.dot(u, wb_ref[k], preferred_element_type=f32)

    @pl.when(pl.program_id(2) == 0)
    def _():
        st_ref[...] = jnp.zeros_like(st_ref)
        for k, w_ref in enumerate(w_refs):
            wb_ref[k] = w_ref[...].astype(bf16)
        project(u_ref, 0, 0)

    lb = lb_ref[...]
    ng = ng_ref[...]
    row = lax.broadcasted_iota(jnp.int32, (rg, rg), 0)
    col = lax.broadcasted_iota(jnp.int32, (rg, rg), 1)
    causal = (row >= col) & ((row // c) == (col // c))
    tril = jnp.where(causal, 1.0, 0.0).astype(bf16)
    zero_blk = jnp.zeros((HEAD, HEAD), bf16)
    nt = (((1,), (1,)), ((), ()))
    tn = (((0,), (0,)), ((), ()))

    def decays(slot, g):
        rows = slice(g * rg, (g + 1) * rg)
        f = lb + (1.0 - lb) * _sigmoid(ps_ref[slot, 1, rows, :])
        b = functools.reduce(lambda x, y: x + y,
                             [jnp.dot(tril, part, preferred_element_type=f32) for part in _split2(jnp.log2(f))])
        return slot, rows, 1.0 - f, b

    def operands(slot, rows, k, b):
        q = ps_ref[slot, 0, rows, :]
        qd, kd, q0, decay, us = [], [], [], [], []
        for ci in range(nck):
            sl = slice(ci * c, (ci + 1) * c)
            bc = b[sl, :]
            b_mid = bc[half - 1:half, :]
            b_last = bc[c - 1:c, :]
            qm = q[sl, :] * jnp.exp2(bc - b_mid)
            qd.append(qm.astype(bf16))
            kd.append((k[sl, :] * jnp.exp2(b_mid - bc)).astype(bf16))
            q0.append((qm * jnp.exp2(b_mid)).astype(bf16))
            decay.append(jnp.exp2(b_last))
            us.append(jnp.exp2(b_last - b_mid))
        return slot, rows, qd, kd, q0, decay, us, ps_ref[slot, 2, rows, :].astype(bf16)

    def intra_chunk(slot, rows, qd, kd, q0, decay, us, v):
        qd_all = jnp.concatenate(qd, axis=0)
        kd_all = jnp.concatenate(kd, axis=0)
        intra = []
        for h in range(2):
            ln = slice(h * HEAD, (h + 1) * HEAD)
            s = lax.dot_general(qd_all[:, ln], kd_all[:, ln], nt, preferred_element_type=f32)
            s = jnp.where(causal, s, 0.0).astype(bf16)
            intra.append(jnp.dot(s, v[:, ln], preferred_element_type=f32))
        upd = [[lax.dot_general(v[ci * c:(ci + 1) * c, h * HEAD:(h + 1) * HEAD],
                                kd[ci][:, h * HEAD:(h + 1) * HEAD], tn, preferred_element_type=f32)
                * us[ci][:, h * HEAD:(h + 1) * HEAD] for h in range(2)] for ci in range(nck)]
        return slot, rows, jnp.concatenate(intra, axis=1), upd, q0, decay

    def recur(out_row0, st, slot, rows, o_intra, upd, q0, decay):
        gate = ps_ref[slot, 3, rows, :]
        for ci in range(nck):
            sl = slice(ci * c, (ci + 1) * c)
            s0 = st[0].astype(bf16)
            s1 = st[1].astype(bf16)
            both = jnp.concatenate([jnp.concatenate([s0, zero_blk], axis=1),
                                    jnp.concatenate([zero_blk, s1], axis=1)], axis=0)
            o = o_intra[sl, :] + lax.dot_general(q0[ci], both, nt, preferred_element_type=f32)
            for h in range(2):
                ln = slice(h * HEAD, (h + 1) * HEAD)
                st[h] = st[h] * decay[ci][:, ln] + upd[ci][h]
                oh = o[:, ln]
                oh = oh * lax.rsqrt(jnp.mean(oh * oh, axis=-1, keepdims=True) + EPS)
                gh = gate[sl, ln]
                r0 = out_row0 + rows.start + ci * c
                o_ref[r0:r0 + c, ln] = (oh * ng[:, ln] * (gh * _sigmoid(gh))).astype(o_ref.dtype)
        return st

    def mix(slot, out_row0, st):
        stage = [decays(slot, g) for g in range(HG_BLOCK // rg)]
        stage = [operands(*x) for x in stage]
        stage = [intra_chunk(*x) for x in stage]
        for x in stage:
            st = recur(out_row0, st, *x)
        return st

    st = [st_ref[0], st_ref[1]]
    project(u_ref, HG_BLOCK, 1)
    st = mix(0, 0, st)
    project(un_ref, 0, 0)
    st = mix(1, HG_BLOCK, st)
    st_ref[0] = st[0]
    st_ref[1] = st[1]


def _hgrn(u, w_in, lb, ng, batch, seq):
    n, dm = u.shape
    w = w_in.shape[1] // 4
    pairs = w // (2 * HEAD)
    nt = seq // HG_TILE
    blk = (HG_TILE, 2 * HEAD)

    def wspec(k):
        return pl.BlockSpec((dm, 2 * HEAD), lambda b, pr, t, k=k: (0, k * pairs + pr))

    vec = pl.BlockSpec((1, 2 * HEAD), lambda b, pr, t: (0, pr))
    return pl.pallas_call(
        _hgrn_kernel,
        out_shape=jax.ShapeDtypeStruct((n, w), bf16),
        grid=(batch, pairs, nt),
        in_specs=[pl.BlockSpec((HG_TILE, dm), lambda b, pr, t: (b * nt + t, 0)),
                  pl.BlockSpec((HG_TILE, dm), lambda b, pr, t: (b * nt + jnp.minimum(t + 1, nt - 1), 0)),
                  wspec(0), wspec(1), wspec(2), wspec(3), vec, vec],
        out_specs=pl.BlockSpec(blk, lambda b, pr, t: (b * nt + t, pr)),
        scratch_shapes=[pltpu.VMEM((2, HEAD, HEAD), f32),
                        pltpu.VMEM((2, 4, HG_BLOCK, 2 * HEAD), f32),
                        pltpu.VMEM((4, dm, 2 * HEAD), bf16)],
        compiler_params=_cparams(("parallel", "parallel", "arbitrary")),
        name="hgrn2",
    )(u, u, w_in, w_in, w_in, w_in, lb.reshape(1, w), ng.reshape(1, w))


MERGE_D = 4
LOG2E = math.log2(math.e)


def _attn_kernel(u_ref, wq_ref, wk_ref, wv_ref, wg_ref, q1_ref, k1_ref, v1_ref, q2_ref, k2_ref, v2_ref,
                 bias_ref, o_ref, ck0, cv0, ck1, cv1, ck2, cv2, oacc, lacc, macc, tmp, nat, p0, w0):
    t = pl.program_id(2)
    par = t % 2
    q_refs = (p0.at[0], q1_ref, q2_ref)
    k_refs = (p0.at[1], k1_ref, k2_ref)
    v_refs = (p0.at[2], v1_ref, v2_ref)
    gate_ref = p0.at[3]
    cks = (ck0, ck1, ck2)
    cvs = (cv0, cv1, cv2)
    scale2 = HEAD ** -0.5 * LOG2E
    blk = ATT_BLOCK
    res_rows = ATT_TILE // MERGE_D
    sub = blk // MERGE_D

    @pl.when(t == 0)
    def _():
        for g, (_, d) in enumerate(DILATED_GROUPS):
            cks[g][:, 2 * blk:, :] = jnp.zeros((d, blk, HEAD), bf16)
            cvs[g][:, 2 * blk:, :] = jnp.zeros((d, blk, HEAD), bf16)
        for k, w_ref in enumerate((wq_ref, wk_ref, wv_ref, wg_ref)):
            w0[:, k * HEAD:(k + 1) * HEAD] = w_ref[...].astype(bf16)

    proj_rows = 512

    def project(c0):
        y = jnp.dot(u_ref[c0:c0 + proj_rows, :], w0[...], preferred_element_type=f32).astype(bf16)
        for k in range(4):
            p0[k, c0:c0 + proj_rows, :] = y[:, k * HEAD:(k + 1) * HEAD]

    slot = pl.ds(pl.multiple_of(par * 2 * blk, blk), blk)

    def stage_carry(g):
        d = DILATED_GROUPS[g][1]
        per_res = ATT_TILE // d
        for r in range(d):
            for src, dst in ((k_refs[g], cks[g]), (v_refs[g], cvs[g])):
                dst[r, blk:2 * blk, :] = src[r * per_res:r * per_res + blk, :]
                dst[r, slot, :] = src[(r + 1) * per_res - blk:(r + 1) * per_res, :]

    cwin = pl.ds(pl.multiple_of((1 - par) * blk, blk), 2 * blk)
    colid = lax.broadcasted_iota(jnp.int32, (blk, 2 * blk), 1)
    pen = jnp.where(colid >= blk, jnp.where(t == 0, NEG, 0.0).astype(f32), 0.0)

    def carry_bias(g):
        b = bias_ref[g]
        swapped = jnp.concatenate([b[:, blk:], b[:, :blk]], axis=1)
        return jnp.where(par == 1, b, swapped) + pen

    def attend(q, kwin, vwin, bias):
        s = lax.dot_general(q, kwin, (((1,), (1,)), ((), ())), preferred_element_type=f32)
        s = s * scale2 + bias
        m = jnp.max(s, axis=-1, keepdims=True)
        p = jnp.exp2(s - m)
        l = jnp.sum(p, axis=-1, keepdims=True)
        acc = jnp.dot(p.astype(bf16), vwin, preferred_element_type=f32)
        return acc, jnp.broadcast_to(l, (blk, HEAD)), jnp.broadcast_to(m, (blk, HEAD))

    def block(g, r, j):
        per_res = ATT_TILE // DILATED_GROUPS[g][1]
        q = q_refs[g][r * per_res + j * blk:r * per_res + (j + 1) * blk, :]
        if j == 0:
            return attend(q, cks[g][r, cwin, :], cvs[g][r, cwin, :], carry_bias(g))
        win = slice(r * per_res + (j - 1) * blk, r * per_res + (j + 1) * blk)
        return attend(q, k_refs[g][win, :], v_refs[g][win, :], bias_ref[g])

    dsts = (oacc, lacc, macc)

    def group1(r, j):
        row0 = r * (ATT_TILE // d1) + j * blk
        for k, val in enumerate(block(1, r, j)):
            dsts[k][1, row0:row0 + blk, :] = val

    def group2(r16):
        rows = pl.ds((r16 % MERGE_D) * res_rows + r16 // MERGE_D, blk, stride=d2 // MERGE_D)
        for k, val in enumerate(block(2, r16, 0)):
            dsts[k][2, rows, :] = val

    def group0(j):
        for k, val in enumerate(block(0, 0, j)):
            tmp[3 * j + k] = val
            for r4 in range(MERGE_D):
                dsts[k][0, r4 * res_rows + j * sub:r4 * res_rows + (j + 1) * sub, :] = (
                    tmp[3 * j + k, pl.ds(r4, sub, stride=MERGE_D), :])

    d1 = DILATED_GROUPS[1][1]
    d2 = DILATED_GROUPS[2][1]
    stage_carry(1)
    stage_carry(2)
    dilated = ([functools.partial(group1, r, j) for r in range(d1) for j in range(ATT_TILE // d1 // blk)]
               + [functools.partial(group2, r16) for r16 in range(d2)])
    n_proj = ATT_TILE // proj_rows
    per_proj = len(dilated) // n_proj
    for i in range(n_proj):
        project(i * proj_rows)
        for task in dilated[i * per_proj:(i + 1) * per_proj]:
            task()
    stage_carry(0)
    for j in range(ATT_TILE // blk):
        group0(j)

    def merge(c, carry):
        for r4 in range(MERGE_D):
            rows = pl.ds(pl.multiple_of(r4 * res_rows + c * blk, blk), blk)
            ms = [macc[g, rows, :] for g in range(N_GROUPS)]
            mx = functools.reduce(jnp.maximum, ms)
            ws = [jnp.exp2(x - mx) for x in ms]
            num = functools.reduce(lambda a, b: a + b, [w * oacc[g, rows, :] for g, w in enumerate(ws)])
            den = functools.reduce(lambda a, b: a + b, [w * lacc[g, rows, :] for g, w in enumerate(ws)])
            nat[pl.ds(c * blk * MERGE_D + r4, blk, stride=MERGE_D), :] = num / den
        return carry

    lax.fori_loop(0, res_rows // blk, merge, 0)

    rc = 256
    for c0 in range(0, ATT_TILE, rc):
        rows = slice(c0, c0 + rc)
        gate = gate_ref[rows, :].astype(f32)
        o_ref[rows, :] = (nat[rows, :] * (gate * _sigmoid(gate))).astype(o_ref.dtype)


def _attention(u, w_in, p, bias, batch, seq, heads):
    n, dm = u.shape
    nt = seq // ATT_TILE
    blk = (ATT_TILE, HEAD)

    def spec(k):
        return pl.BlockSpec(blk, lambda b, h, t, k=k: (b * nt + t, k * heads + h))

    def wspec(k):
        return pl.BlockSpec((dm, HEAD), lambda b, h, t, k=k: (0, k * heads + h))

    scratch = []
    for _, d in DILATED_GROUPS:
        scratch += [pltpu.VMEM((d, 3 * ATT_BLOCK, HEAD), bf16)] * 2
    scratch += [pltpu.VMEM((N_GROUPS, ATT_TILE, HEAD), f32)] * 3
    scratch += [pltpu.VMEM((3 * ATT_TILE // ATT_BLOCK, ATT_BLOCK, HEAD), f32), pltpu.VMEM((ATT_TILE, HEAD), f32)]
    scratch += [pltpu.VMEM((4, ATT_TILE, HEAD), bf16), pltpu.VMEM((dm, 4 * HEAD), bf16)]
    return pl.pallas_call(
        _attn_kernel,
        out_shape=jax.ShapeDtypeStruct((n, heads * HEAD), bf16),
        grid=(batch, heads, nt),
        in_specs=[pl.BlockSpec((ATT_TILE, dm), lambda b, h, t: (b * nt + t, 0)),
                  wspec(0), wspec(1), wspec(2), wspec(3 * N_GROUPS)] + [spec(k) for k in range(6)] + [
                  pl.BlockSpec((N_GROUPS, None, ATT_BLOCK, 2 * ATT_BLOCK), lambda b, h, t: (0, h, 0, 0))],
        out_specs=pl.BlockSpec(blk, lambda b, h, t: (b * nt + t, h)),
        scratch_shapes=scratch,
        compiler_params=_cparams(("parallel", "parallel", "arbitrary")),
        name="dilated_attention",
    )(u, w_in, w_in, w_in, w_in, *([p] * 6), bias)


def _t5_bucket(dist):
    max_exact = N_BUCKETS // 2
    df = jnp.maximum(dist, 1).astype(f32)
    large = max_exact + (jnp.log(df / max_exact) / math.log(MAX_DISTANCE / max_exact)
                         * (N_BUCKETS - max_exact)).astype(jnp.int32)
    large = jnp.minimum(large, N_BUCKETS - 1)
    return jnp.where(dist < max_exact, dist, large)


def _bias_tables(rel_bias, heads):
    a = jnp.arange(ATT_BLOCK)[:, None]
    c = jnp.arange(2 * ATT_BLOCK)[None, :]
    rel = ATT_BLOCK + a - c
    tables = []
    for g, (window, d) in enumerate(DILATED_GROUPS):
        assert window // d == ATT_BLOCK and ATT_TILE % (d * ATT_BLOCK) == 0
        band = (rel >= 0) & (rel <= window // d)
        onehot = jax.nn.one_hot(_t5_bucket(jnp.maximum(rel, 0) * d), N_BUCKETS, dtype=f32)
        tab = jnp.einsum("acb,bh->hac", onehot, rel_bias[:, g * heads:(g + 1) * heads].astype(f32),
                         precision=lax.Precision.HIGHEST)
        tables.append(jnp.where(band[None], tab * LOG2E, NEG))
    return jnp.stack(tables, axis=0)


PERM_ROWS = 256


def _groupproj_kernel(a_ref, w_ref, o_ref, perm_ref, *, col_blocks_per_group):
    j = pl.program_id(1)
    tm = a_ref.shape[0]

    @pl.when(j == 0)
    def _():
        i = lax.broadcasted_iota(jnp.int32, (PERM_ROWS, PERM_ROWS), 0)
        k = lax.broadcasted_iota(jnp.int32, (PERM_ROWS, PERM_ROWS), 1)
        for g, (_, d) in enumerate(DILATED_GROUPS[1:]):
            per = PERM_ROWS // d
            perm = jnp.where(k == (i % per) * d + i // per, 1.0, 0.0).astype(bf16)
            for sb in range(tm // PERM_ROWS):
                y = jnp.dot(perm, a_ref[sb * PERM_ROWS:(sb + 1) * PERM_ROWS, :],
                            preferred_element_type=f32).astype(bf16)
                for r in range(d):
                    dst = r * (tm // d) + sb * per
                    perm_ref[g, dst:dst + per, :] = y[r * per:(r + 1) * per, :]

    def project(lhs_ref):
        w = w_ref[...].astype(bf16)
        rc = 512
        for r in range(tm // rc):
            rows = slice(r * rc, (r + 1) * rc)
            o_ref[rows, :] = jnp.dot(lhs_ref[rows, :], w, preferred_element_type=f32).astype(o_ref.dtype)

    grp = j // col_blocks_per_group
    for g in range(N_GROUPS - 1):
        pl.when(grp == g)(functools.partial(project, perm_ref.at[g]))


def _groupproj(u, w, width, tn=1024):
    n, k = u.shape
    c = 3 * (N_GROUPS - 1) * width
    tn = math.gcd(tn, width)
    first = 3 * width // tn
    return pl.pallas_call(
        functools.partial(_groupproj_kernel, col_blocks_per_group=3 * width // tn),
        out_shape=jax.ShapeDtypeStruct((n, c), bf16),
        grid=(n // ATT_TILE, c // tn),
        in_specs=[pl.BlockSpec((ATT_TILE, k), lambda i, j: (i, 0)),
                  pl.BlockSpec((k, tn), lambda i, j: (0, j + first))],
        out_specs=pl.BlockSpec((ATT_TILE, tn), lambda i, j: (i, j)),
        scratch_shapes=[pltpu.VMEM((N_GROUPS - 1, ATT_TILE, k), bf16)],
        compiler_params=_cparams(("parallel", "arbitrary")),
        name="group_proj",
    )(u, w)


def kernel(x, ln_g, hg_w_in, hg_lb_logits, hg_norm_g, hg_w_out, att_w_in, att_w_out, rel_bias, final_g):
    batch, seq, d_model = x.shape
    n = batch * seq
    w = hg_w_out.shape[1]
    heads = w // HEAD
    assert seq % ATT_TILE == 0 and seq % HG_TILE == 0 and w % (2 * HEAD) == 0

    lower = jnp.cumsum(jax.nn.softmax(hg_lb_logits.astype(f32), axis=0), axis=0)
    h0 = x.reshape(n, d_model)

    u0 = _rmsnorm(h0, ln_g[0], bf16)
    y0 = _hgrn(u0, hg_w_in[0], lower[0], hg_norm_g[0], batch, seq)
    h1, u1 = _outproj(y0, hg_w_out[0], h0, ln_g[1], bf16, emit_h=True)

    wa = att_w_in[0]
    y1 = _attention(u1, wa, _groupproj(u1, wa, w), _bias_tables(rel_bias, heads), batch, seq, heads)
    (out,) = _outproj(y1, att_w_out[0], h1, final_g, f32, emit_h=False)
    return out.reshape(batch, seq, d_model)
```

```python
import functools
import math

import jax
import jax.numpy as jnp
from jax import lax
from jax.experimental import pallas as pl
from jax.experimental.pallas import tpu as pltpu

EPS = 1e-6
HEAD = 128
HG_CHUNK = 64
ATT_BLOCK = 128
DILATED_GROUPS = ((128, 1), (512, 4), (2048, 16))
N_GROUPS = len(DILATED_GROUPS)
N_BUCKETS = 32
MAX_DISTANCE = 2048
ATT_TILE = ATT_BLOCK * max(d for _, d in DILATED_GROUPS)
NEG = -1e30

VMEM_LIMIT = 56 * 1024 * 1024

f32 = jnp.float32
bf16 = jnp.bfloat16


def _cparams(sem):
    return pltpu.CompilerParams(dimension_semantics=sem, vmem_limit_bytes=VMEM_LIMIT)


def _rmsnorm_kernel(x_ref, g_ref, o_ref):
    x = x_ref[...]
    ms = jnp.mean(x * x, axis=-1, keepdims=True)
    o_ref[...] = (x * lax.rsqrt(ms + EPS) * g_ref[...]).astype(o_ref.dtype)


def _rmsnorm(x, g, out_dtype, tm=1024):
    n, d = x.shape
    return pl.pallas_call(
        _rmsnorm_kernel,
        out_shape=jax.ShapeDtypeStruct((n, d), out_dtype),
        grid=(n // tm,),
        in_specs=[pl.BlockSpec((tm, d), lambda i: (i, 0)),
                  pl.BlockSpec((1, d), lambda i: (0, 0))],
        out_specs=pl.BlockSpec((tm, d), lambda i: (i, 0)),
        compiler_params=_cparams(("parallel",)),
        name="rmsnorm",
    )(x, g.reshape(1, d))


def _matmul_kernel(a_ref, w_ref, o_ref, *, row_chunk):
    tm = a_ref.shape[0]
    for r in range(tm // row_chunk):
        rows = slice(r * row_chunk, (r + 1) * row_chunk)
        o_ref[rows, :] = jnp.dot(a_ref[rows, :], w_ref[...], preferred_element_type=f32).astype(o_ref.dtype)


def _matmul(a, w, out_dtype, tm, tn):
    n, k = a.shape
    c = w.shape[1]
    tm = min(tm, n)
    tn = math.gcd(tn, c)
    return pl.pallas_call(
        functools.partial(_matmul_kernel, row_chunk=min(512, tm)),
        out_shape=jax.ShapeDtypeStruct((n, c), out_dtype),
        grid=(n // tm, c // tn),
        in_specs=[pl.BlockSpec((tm, k), lambda i, j: (i, 0)),
                  pl.BlockSpec((k, tn), lambda i, j: (0, j))],
        out_specs=pl.BlockSpec((tm, tn), lambda i, j: (i, j)),
        compiler_params=_cparams(("parallel", "arbitrary")),
        name="proj_matmul",
    )(a, w)


def _outproj_kernel(y_ref, w_ref, h_ref, g_ref, *refs, emit_h):
    *out_refs, wb_ref = refs

    @pl.when(pl.program_id(0) == 0)
    def _():
        wb_ref[...] = w_ref[...].astype(bf16)

    h = h_ref[...] + jnp.dot(y_ref[...], wb_ref[...], preferred_element_type=f32)
    if emit_h:
        out_refs[0][...] = h
    n_ref = out_refs[-1]
    ms = jnp.mean(h * h, axis=-1, keepdims=True)
    n_ref[...] = (h * lax.rsqrt(ms + EPS) * g_ref[...]).astype(n_ref.dtype)


def _outproj(y, w, h, g, norm_dtype, emit_h, tm=1024):
    n, k = y.shape
    d = w.shape[1]
    tm = min(tm, n)
    row = lambda i: (i, 0)
    out_shape = [jax.ShapeDtypeStruct((n, d), norm_dtype)]
    out_specs = [pl.BlockSpec((tm, d), row)]
    if emit_h:
        out_shape.insert(0, jax.ShapeDtypeStruct((n, d), f32))
        out_specs.insert(0, pl.BlockSpec((tm, d), row))
    return pl.pallas_call(
        functools.partial(_outproj_kernel, emit_h=emit_h),
        out_shape=out_shape,
        grid=(n // tm,),
        in_specs=[pl.BlockSpec((tm, k), row),
                  pl.BlockSpec((k, d), lambda i: (0, 0)),
                  pl.BlockSpec((tm, d), row),
                  pl.BlockSpec((1, d), lambda i: (0, 0))],
        out_specs=out_specs,
        scratch_shapes=[pltpu.VMEM((k, d), bf16)],
        compiler_params=_cparams(("arbitrary",)),
        name="outproj",
    )(y, w, h, g.reshape(1, d))


HG_GROUP = 4 * HG_CHUNK
HG_BLOCK = 2 * HG_GROUP
HG_TILE = 4 * HG_BLOCK


def _sigmoid(x):
    return 1.0 / (1.0 + jnp.exp(-x))


def _split2(x):
    hi = x.astype(bf16)
    return hi, (x - hi.astype(f32)).astype(bf16)


def _hgrn_kernel(u_ref, un_ref, wq_ref, wf_ref, wi_ref, wg_ref, lb_ref, ng_ref, o_ref, st_ref, ps_ref, wb_ref):
    c = HG_CHUNK
    half = c // 2
    rg = HG_GROUP
    nck = rg // c
    w_refs = (wq_ref, wf_ref, wi_ref, wg_ref)

    def project(src_ref, row0, slot):
        u = src_ref[row0:row0 + HG_BLOCK, :]
        for k in range(len(w_refs)):
            ps_ref[slot, k] = jnp.dot(u, wb_ref[k], preferred_element_type=f32)

    @pl.when(pl.program_id(2) == 0)
    def _():
        st_ref[...] = jnp.zeros_like(st_ref)
        for k, w_ref in enumerate(w_refs):
            wb_ref[k] = w_ref[...].astype(bf16)
        project(u_ref, 0, 0)

    lb = lb_ref[...]
    ng = ng_ref[...]
    row = lax.broadcasted_iota(jnp.int32, (rg, rg), 0)
    col = lax.broadcasted_iota(jnp.int32, (rg, rg), 1)
    causal = (row >= col) & ((row // c) == (col // c))
    tril = jnp.where(causal, 1.0, 0.0).astype(bf16)
    zero_blk = jnp.zeros((HEAD, HEAD), bf16)
    nt = (((1,), (1,)), ((), ()))
    tn = (((0,), (0,)), ((), ()))

    def decays(slot, g):
        rows = slice(g * rg, (g + 1) * rg)
        f = lb + (1.0 - lb) * _sigmoid(ps_ref[slot, 1, rows, :])
        b = functools.reduce(lambda x, y: x + y,
                             [jnp.dot(tril, part, preferred_element_type=f32) for part in _split2(jnp.log2(f))])
        return slot, rows, 1.0 - f, b

    def operands(slot, rows, k, b):
        q = ps_ref[slot, 0, rows, :]
        qd, kd, q0, decay, us = [], [], [], [], []
        for ci in range(nck):
            sl = slice(ci * c, (ci + 1) * c)
            bc = b[sl, :]
            b_mid = bc[half - 1:half, :]
            b_last = bc[c - 1:c, :]
            qm = q[sl, :] * jnp.exp2(bc - b_mid)
            qd.append(qm.astype(bf16))
            kd.append((k[sl, :] * jnp.exp2(b_mid - bc)).astype(bf16))
            q0.append((qm * jnp.exp2(b_mid)).astype(bf16))
            decay.append(jnp.exp2(b_last))
            us.append(jnp.exp2(b_last - b_mid))
        return slot, rows, qd, kd, q0, decay, us, ps_ref[slot, 2, rows, :].astype(bf16)

    def intra_chunk(slot, rows, qd, kd, q0, decay, us, v):
        qd_all = jnp.concatenate(qd, axis=0)
        kd_all = jnp.concatenate(kd, axis=0)
        intra = []
        for h in range(2):
            ln = slice(h * HEAD, (h + 1) * HEAD)
            s = lax.dot_general(qd_all[:, ln], kd_all[:, ln], nt, preferred_element_type=f32)
            s = jnp.where(causal, s, 0.0).astype(bf16)
            intra.append(jnp.dot(s, v[:, ln], preferred_element_type=f32))
        upd = [[lax.dot_general(v[ci * c:(ci + 1) * c, h * HEAD:(h + 1) * HEAD],
                                kd[ci][:, h * HEAD:(h + 1) * HEAD], tn, preferred_element_type=f32)
                * us[ci][:, h * HEAD:(h + 1) * HEAD] for h in range(2)] for ci in range(nck)]
        return slot, rows, jnp.concatenate(intra, axis=1), upd, q0, decay

    def recur(out_row0, st, slot, rows, o_intra, upd, q0, decay):
        gate = ps_ref[slot, 3, rows, :]
        for ci in range(nck):
            sl = slice(ci * c, (ci + 1) * c)
            s0 = st[0].astype(bf16)
            s1 = st[1].astype(bf16)
            both = jnp.concatenate([jnp.concatenate([s0, zero_blk], axis=1),
                                    jnp.concatenate([zero_blk, s1], axis=1)], axis=0)
            o = o_intra[sl, :] + lax.dot_general(q0[ci], both, nt, preferred_element_type=f32)
            for h in range(2):
                ln = slice(h * HEAD, (h + 1) * HEAD)
                st[h] = st[h] * decay[ci][:, ln] + upd[ci][h]
                oh = o[:, ln]
                oh = oh * lax.rsqrt(jnp.mean(oh * oh, axis=-1, keepdims=True) + EPS)
                gh = gate[sl, ln]
                r0 = out_row0 + rows.start + ci * c
                o_ref[r0:r0 + c, ln] = (oh * ng[:, ln] * (gh * _sigmoid(gh))).astype(o_ref.dtype)
        return st

    def mix(slot, out_row0, st):
        stage = [decays(slot, g) for g in range(HG_BLOCK // rg)]
        stage = [operands(*x) for x in stage]
        stage = [intra_chunk(*x) for x in stage]
        for x in stage:
            st = recur(out_row0, st, *x)
        return st

    st = [st_ref[0], st_ref[1]]
    n_blocks = HG_TILE // HG_BLOCK
    for i in range(n_blocks):
        if i + 1 < n_blocks:
            project(u_ref, (i + 1) * HG_BLOCK, (i + 1) % 2)
        else:
            project(un_ref, 0, 0)
        st = mix(i % 2, i * HG_BLOCK, st)
    st_ref[0] = st[0]
    st_ref[1] = st[1]


def _hgrn(u, w_in, lb, ng, batch, seq):
    n, dm = u.shape
    w = w_in.shape[1] // 4
    pairs = w // (2 * HEAD)
    nt = seq // HG_TILE
    blk = (HG_TILE, 2 * HEAD)

    def wspec(k):
        return pl.BlockSpec((dm, 2 * HEAD), lambda b, pr, t, k=k: (0, k * pairs + pr))

    vec = pl.BlockSpec((1, 2 * HEAD), lambda b, pr, t: (0, pr))
    return pl.pallas_call(
        _hgrn_kernel,
        out_shape=jax.ShapeDtypeStruct((n, w), bf16),
        grid=(batch, pairs, nt),
        in_specs=[pl.BlockSpec((HG_TILE, dm), lambda b, pr, t: (b * nt + t, 0)),
                  pl.BlockSpec((HG_TILE, dm), lambda b, pr, t: (b * nt + jnp.minimum(t + 1, nt - 1), 0)),
                  wspec(0), wspec(1), wspec(2), wspec(3), vec, vec],
        out_specs=pl.BlockSpec(blk, lambda b, pr, t: (b * nt + t, pr)),
        scratch_shapes=[pltpu.VMEM((2, HEAD, HEAD), f32),
                        pltpu.VMEM((2, 4, HG_BLOCK, 2 * HEAD), f32),
                        pltpu.VMEM((4, dm, 2 * HEAD), bf16)],
        compiler_params=_cparams(("parallel", "parallel", "arbitrary")),
        name="hgrn2",
    )(u, u, w_in, w_in, w_in, w_in, lb.reshape(1, w), ng.reshape(1, w))


MERGE_D = 4
LOG2E = math.log2(math.e)


def _attn_kernel(u_ref, wq_ref, wk_ref, wv_ref, wg_ref, q1_ref, k1_ref, v1_ref, q2_ref, k2_ref, v2_ref,
                 bias_ref, o_ref, ck0, cv0, ck1, cv1, ck2, cv2, oacc, lacc, macc, tmp, nat, p0, w0):
    t = pl.program_id(2)
    par = t % 2
    q_refs = (p0.at[0], q1_ref, q2_ref)
    k_refs = (p0.at[1], k1_ref, k2_ref)
    v_refs = (p0.at[2], v1_ref, v2_ref)
    gate_ref = p0.at[3]
    cks = (ck0, ck1, ck2)
    cvs = (cv0, cv1, cv2)
    scale2 = HEAD ** -0.5 * LOG2E
    blk = ATT_BLOCK
    res_rows = ATT_TILE // MERGE_D
    sub = blk // MERGE_D

    @pl.when(t == 0)
    def _():
        for g, (_, d) in enumerate(DILATED_GROUPS):
            cks[g][:, 2 * blk:, :] = jnp.zeros((d, blk, HEAD), bf16)
            cvs[g][:, 2 * blk:, :] = jnp.zeros((d, blk, HEAD), bf16)
        for k, w_ref in enumerate((wq_ref, wk_ref, wv_ref, wg_ref)):
            w0[:, k * HEAD:(k + 1) * HEAD] = w_ref[...].astype(bf16)

    proj_rows = 512

    def project(c0):
        y = jnp.dot(u_ref[c0:c0 + proj_rows, :], w0[...], preferred_element_type=f32).astype(bf16)
        for k in range(4):
            p0[k, c0:c0 + proj_rows, :] = y[:, k * HEAD:(k + 1) * HEAD]

    slot = pl.ds(pl.multiple_of(par * 2 * blk, blk), blk)

    def stage_carry(g):
        d = DILATED_GROUPS[g][1]
        per_res = ATT_TILE // d
        for r in range(d):
            for src, dst in ((k_refs[g], cks[g]), (v_refs[g], cvs[g])):
                dst[r, blk:2 * blk, :] = src[r * per_res:r * per_res + blk, :]
                dst[r, slot, :] = src[(r + 1) * per_res - blk:(r + 1) * per_res, :]

    cwin = pl.ds(pl.multiple_of((1 - par) * blk, blk), 2 * blk)
    colid = lax.broadcasted_iota(jnp.int32, (blk, 2 * blk), 1)
    pen = jnp.where(colid >= blk, jnp.where(t == 0, NEG, 0.0).astype(f32), 0.0)

    def carry_bias(g):
        b = bias_ref[g]
        swapped = jnp.concatenate([b[:, blk:], b[:, :blk]], axis=1)
        return jnp.where(par == 1, b, swapped) + pen

    def attend(q, kwin, vwin, bias):
        s = lax.dot_general(q, kwin, (((1,), (1,)), ((), ())), preferred_element_type=f32)
        s = s * scale2 + bias
        m = jnp.max(s, axis=-1, keepdims=True)
        p = jnp.exp2(s - m)
        l = jnp.sum(p, axis=-1, keepdims=True)
        acc = jnp.dot(p.astype(bf16), vwin, preferred_element_type=f32)
        return acc, jnp.broadcast_to(l, (blk, HEAD)), jnp.broadcast_to(m, (blk, HEAD))

    def block(g, r, j):
        per_res = ATT_TILE // DILATED_GROUPS[g][1]
        q = q_refs[g][r * per_res + j * blk:r * per_res + (j + 1) * blk, :]
        if j == 0:
            return attend(q, cks[g][r, cwin, :], cvs[g][r, cwin, :], carry_bias(g))
        win = slice(r * per_res + (j - 1) * blk, r * per_res + (j + 1) * blk)
        return attend(q, k_refs[g][win, :], v_refs[g][win, :], bias_ref[g])

    dsts = (oacc, lacc, macc)

    def group1(r, j):
        row0 = r * (ATT_TILE // d1) + j * blk
        for k, val in enumerate(block(1, r, j)):
            dsts[k][1, row0:row0 + blk, :] = val

    def group2(r16):
        rows = pl.ds((r16 % MERGE_D) * res_rows + r16 // MERGE_D, blk, stride=d2 // MERGE_D)
        for k, val in enumerate(block(2, r16, 0)):
            dsts[k][2, rows, :] = val

    def group0(j):
        for k, val in enumerate(block(0, 0, j)):
            tmp[3 * j + k] = val
            for r4 in range(MERGE_D):
                dsts[k][0, r4 * res_rows + j * sub:r4 * res_rows + (j + 1) * sub, :] = (
                    tmp[3 * j + k, pl.ds(r4, sub, stride=MERGE_D), :])

    d1 = DILATED_GROUPS[1][1]
    d2 = DILATED_GROUPS[2][1]
    stage_carry(1)
    stage_carry(2)
    dilated = ([functools.partial(group1, r, j) for r in range(d1) for j in range(ATT_TILE // d1 // blk)]
               + [functools.partial(group2, r16) for r16 in range(d2)])
    n_proj = ATT_TILE // proj_rows
    per_proj = len(dilated) // n_proj
    for i in range(n_proj):
        project(i * proj_rows)
        for task in dilated[i * per_proj:(i + 1) * per_proj]:
            task()
    stage_carry(0)
    for j in range(ATT_TILE // blk):
        group0(j)

    def merge(c, carry):
        for r4 in range(MERGE_D):
            rows = pl.ds(pl.multiple_of(r4 * res_rows + c * blk, blk), blk)
            ms = [macc[g, rows, :] for g in range(N_GROUPS)]
            mx = functools.reduce(jnp.maximum, ms)
            ws = [jnp.exp2(x - mx) for x in ms]
            num = functools.reduce(lambda a, b: a + b, [w * oacc[g, rows, :] for g, w in enumerate(ws)])
            den = functools.reduce(lambda a, b: a + b, [w * lacc[g, rows, :] for g, w in enumerate(ws)])
            nat[pl.ds(c * blk * MERGE_D + r4, blk, stride=MERGE_D), :] = num / den
        return carry

    lax.fori_loop(0, res_rows // blk, merge, 0)

    rc = 256
    for c0 in range(0, ATT_TILE, rc):
        rows = slice(c0, c0 + rc)
        gate = gate_ref[rows, :].astype(f32)
        o_ref[rows, :] = (nat[rows, :] * (gate * _sigmoid(gate))).astype(o_ref.dtype)


def _attention(u, w_in, p, bias, batch, seq, heads):
    n, dm = u.shape
    nt = seq // ATT_TILE
    blk = (ATT_TILE, HEAD)

    def spec(k):
        return pl.BlockSpec(blk, lambda b, h, t, k=k: (b * nt + t, k * heads + h))

    def wspec(k):
        return pl.BlockSpec((dm, HEAD), lambda b, h, t, k=k: (0, k * heads + h))

    scratch = []
    for _, d in DILATED_GROUPS:
        scratch += [pltpu.VMEM((d, 3 * ATT_BLOCK, HEAD), bf16)] * 2
    scratch += [pltpu.VMEM((N_GROUPS, ATT_TILE, HEAD), f32)] * 3
    scratch += [pltpu.VMEM((3 * ATT_TILE // ATT_BLOCK, ATT_BLOCK, HEAD), f32), pltpu.VMEM((ATT_TILE, HEAD), f32)]
    scratch += [pltpu.VMEM((4, ATT_TILE, HEAD), bf16), pltpu.VMEM((dm, 4 * HEAD), bf16)]
    return pl.pallas_call(
        _attn_kernel,
        out_shape=jax.ShapeDtypeStruct((n, heads * HEAD), bf16),
        grid=(batch, heads, nt),
        in_specs=[pl.BlockSpec((ATT_TILE, dm), lambda b, h, t: (b * nt + t, 0)),
                  wspec(0), wspec(1), wspec(2), wspec(3 * N_GROUPS)] + [spec(k) for k in range(6)] + [
                  pl.BlockSpec((N_GROUPS, None, ATT_BLOCK, 2 * ATT_BLOCK), lambda b, h, t: (0, h, 0, 0))],
        out_specs=pl.BlockSpec(blk, lambda b, h, t: (b * nt + t, h)),
        scratch_shapes=scratch,
        compiler_params=_cparams(("parallel", "parallel", "arbitrary")),
        name="dilated_attention",
    )(u, w_in, w_in, w_in, w_in, *([p] * 6), bias)


def _t5_bucket(dist):
    max_exact = N_BUCKETS // 2
    df = jnp.maximum(dist, 1).astype(f32)
    large = max_exact + (jnp.log(df / max_exact) / math.log(MAX_DISTANCE / max_exact)
                         * (N_BUCKETS - max_exact)).astype(jnp.int32)
    large = jnp.minimum(large, N_BUCKETS - 1)
    return jnp.where(dist < max_exact, dist, large)


def _bias_tables(rel_bias, heads):
    a = jnp.arange(ATT_BLOCK)[:, None]
    c = jnp.arange(2 * ATT_BLOCK)[None, :]
    rel = ATT_BLOCK + a - c
    tables = []
    for g, (window, d) in enumerate(DILATED_GROUPS):
        assert window // d == ATT_BLOCK and ATT_TILE % (d * ATT_BLOCK) == 0
        band = (rel >= 0) & (rel <= window // d)
        onehot = jax.nn.one_hot(_t5_bucket(jnp.maximum(rel, 0) * d), N_BUCKETS, dtype=f32)
        tab = jnp.einsum("acb,bh->hac", onehot, rel_bias[:, g * heads:(g + 1) * heads].astype(f32),
                         precision=lax.Precision.HIGHEST)
        tables.append(jnp.where(band[None], tab * LOG2E, NEG))
    return jnp.stack(tables, axis=0)


PERM_ROWS = 256


def _groupproj_kernel(a_ref, w_ref, o_ref, perm_ref, *, col_blocks_per_group):
    j = pl.program_id(1)
    tm = a_ref.shape[0]

    @pl.when(j == 0)
    def _():
        i = lax.broadcasted_iota(jnp.int32, (PERM_ROWS, PERM_ROWS), 0)
        k = lax.broadcasted_iota(jnp.int32, (PERM_ROWS, PERM_ROWS), 1)
        for g, (_, d) in enumerate(DILATED_GROUPS[1:]):
            per = PERM_ROWS // d
            perm = jnp.where(k == (i % per) * d + i // per, 1.0, 0.0).astype(bf16)
            for sb in range(tm // PERM_ROWS):
                y = jnp.dot(perm, a_ref[sb * PERM_ROWS:(sb + 1) * PERM_ROWS, :],
                            preferred_element_type=f32).astype(bf16)
                for r in range(d):
                    dst = r * (tm // d) + sb * per
                    perm_ref[g, dst:dst + per, :] = y[r * per:(r + 1) * per, :]

    def project(lhs_ref):
        w = w_ref[...].astype(bf16)
        rc = 512
        for r in range(tm // rc):
            rows = slice(r * rc, (r + 1) * rc)
            o_ref[rows, :] = jnp.dot(lhs_ref[rows, :], w, preferred_element_type=f32).astype(o_ref.dtype)

    grp = j // col_blocks_per_group
    for g in range(N_GROUPS - 1):
        pl.when(grp == g)(functools.partial(project, perm_ref.at[g]))


def _groupproj(u, w, width, tn=2048):
    n, k = u.shape
    c = 3 * (N_GROUPS - 1) * width
    tn = math.gcd(tn, width)
    first = 3 * width // tn
    return pl.pallas_call(
        functools.partial(_groupproj_kernel, col_blocks_per_group=3 * width // tn),
        out_shape=jax.ShapeDtypeStruct((n, c), bf16),
        grid=(n // ATT_TILE, c // tn),
        in_specs=[pl.BlockSpec((ATT_TILE, k), lambda i, j: (i, 0)),
                  pl.BlockSpec((k, tn), lambda i, j: (0, j + first))],
        out_specs=pl.BlockSpec((ATT_TILE, tn), lambda i, j: (i, j)),
        scratch_shapes=[pltpu.VMEM((N_GROUPS - 1, ATT_TILE, k), bf16)],
        compiler_params=_cparams(("parallel", "arbitrary")),
        name="group_proj",
    )(u, w)


def kernel(x, ln_g, hg_w_in, hg_lb_logits, hg_norm_g, hg_w_out, att_w_in, att_w_out, rel_bias, final_g):
    batch, seq, d_model = x.shape
    n = batch * seq
    w = hg_w_out.shape[1]
    heads = w // HEAD
    assert seq % ATT_TILE == 0 and seq % HG_TILE == 0 and w % (2 * HEAD) == 0

    lower = jnp.cumsum(jax.nn.softmax(hg_lb_logits.astype(f32), axis=0), axis=0)
    h0 = x.reshape(n, d_model)

    u0 = _rmsnorm(h0, ln_g[0], bf16)
    y0 = _hgrn(u0, hg_w_in[0], lower[0], hg_norm_g[0], batch, seq)
    h1, u1 = _outproj(y0, hg_w_out[0], h0, ln_g[1], bf16, emit_h=True)

    wa = att_w_in[0]
    y1 = _attention(u1, wa, _groupproj(u1, wa, w), _bias_tables(rel_bias, heads), batch, seq, heads)
    (out,) = _outproj(y1, att_w_out[0], h1, final_g, f32, emit_h=False)
    return out.reshape(batch, seq, d_model)
```

```python
import functools
import math

import jax
import jax.numpy as jnp
from jax import lax
from jax.experimental import pallas as pl
from jax.experimental.pallas import tpu as pltpu

EPS = 1e-6
HEAD = 128
HG_CHUNK = 64
ATT_BLOCK = 128
DILATED_GROUPS = ((128, 1), (512, 4), (2048, 16))
N_GROUPS = len(DILATED_GROUPS)
N_BUCKETS = 32
MAX_DISTANCE = 2048
ATT_TILE = ATT_BLOCK * max(d for _, d in DILATED_GROUPS)
NEG = -1e30

VMEM_LIMIT = 56 * 1024 * 1024
ROW_CHUNK = 512

f32 = jnp.float32
bf16 = jnp.bfloat16


def _cparams(sem):
    return pltpu.CompilerParams(dimension_semantics=sem, vmem_limit_bytes=VMEM_LIMIT)


def _rmsnorm_kernel(x_ref, g_ref, o_ref):
    x = x_ref[...]
    ms = jnp.mean(x * x, axis=-1, keepdims=True)
    o_ref[...] = (x * lax.rsqrt(ms + EPS) * g_ref[...]).astype(o_ref.dtype)


def _rmsnorm(x, g, out_dtype, tm=1024):
    n, d = x.shape
    return pl.pallas_call(
        _rmsnorm_kernel,
        out_shape=jax.ShapeDtypeStruct((n, d), out_dtype),
        grid=(n // tm,),
        in_specs=[pl.BlockSpec((tm, d), lambda i: (i, 0)),
                  pl.BlockSpec((1, d), lambda i: (0, 0))],
        out_specs=pl.BlockSpec((tm, d), lambda i: (i, 0)),
        compiler_params=_cparams(("parallel",)),
        name="rmsnorm",
    )(x, g.reshape(1, d))


def _outproj_kernel(y_ref, w_ref, h_ref, g_ref, *refs, emit_h):
    *out_refs, wb_ref = refs

    @pl.when(pl.program_id(0) == 0)
    def _():
        wb_ref[...] = w_ref[...].astype(bf16)

    h = h_ref[...] + jnp.dot(y_ref[...], wb_ref[...], preferred_element_type=f32)
    if emit_h:
        out_refs[0][...] = h
    n_ref = out_refs[-1]
    ms = jnp.mean(h * h, axis=-1, keepdims=True)
    n_ref[...] = (h * lax.rsqrt(ms + EPS) * g_ref[...]).astype(n_ref.dtype)


def _outproj(y, w, h, g, norm_dtype, emit_h, tm=1024):
    n, k = y.shape
    d = w.shape[1]
    tm = min(tm, n)
    row = lambda i: (i, 0)
    out_shape = [jax.ShapeDtypeStruct((n, d), norm_dtype)]
    out_specs = [pl.BlockSpec((tm, d), row)]
    if emit_h:
        out_shape.insert(0, jax.ShapeDtypeStruct((n, d), f32))
        out_specs.insert(0, pl.BlockSpec((tm, d), row))
    return pl.pallas_call(
        functools.partial(_outproj_kernel, emit_h=emit_h),
        out_shape=out_shape,
        grid=(n // tm,),
        in_specs=[pl.BlockSpec((tm, k), row),
                  pl.BlockSpec((k, d), lambda i: (0, 0)),
                  pl.BlockSpec((tm, d), row),
                  pl.BlockSpec((1, d), lambda i: (0, 0))],
        out_specs=out_specs,
        scratch_shapes=[pltpu.VMEM((k, d), bf16)],
        compiler_params=_cparams(("arbitrary",)),
        name="outproj",
    )(y, w, h, g.reshape(1, d))


HG_GROUP = 4 * HG_CHUNK
HG_BLOCK = 2 * HG_GROUP
HG_TILE = 4 * HG_BLOCK


def _sigmoid(x):
    return 1.0 / (1.0 + jnp.exp(-x))


def _split2(x):
    hi = x.astype(bf16)
    return hi, (x - hi.astype(f32)).astype(bf16)


def _hgrn_kernel(u_ref, un_ref, wq_ref, wf_ref, wi_ref, wg_ref, lb_ref, ng_ref, o_ref, st_ref, ps_ref, wb_ref):
    c = HG_CHUNK
    half = c // 2
    rg = HG_GROUP
    nck = rg // c
    w_refs = (wq_ref, wf_ref, wi_ref, wg_ref)

    def project(src_ref, row0, slot):
        u = src_ref[row0:row0 + HG_BLOCK, :]
        for k in range(len(w_refs)):
            ps_ref[slot, k] = jnp.dot(u, wb_ref[k], preferred_element_type=f32)

    @pl.when(pl.program_id(2) == 0)
    def _():
        st_ref[...] = jnp.zeros_like(st_ref)
        for k, w_ref in enumerate(w_refs):
            wb_ref[k] = w_ref[...].astype(bf16)
        project(u_ref, 0, 0)

    lb = lb_ref[...]
    ng = ng_ref[...]
    row = lax.broadcasted_iota(jnp.int32, (rg, rg), 0)
    col = lax.broadcasted_iota(jnp.int32, (rg, rg), 1)
    causal = (row >= col) & ((row // c) == (col // c))
    tril = jnp.where(causal, 1.0, 0.0).astype(bf16)
    zero_blk = jnp.zeros((HEAD, HEAD), bf16)
    nt = (((1,), (1,)), ((), ()))
    tn = (((0,), (0,)), ((), ()))

    def decays(slot, g):
        rows = slice(g * rg, (g + 1) * rg)
        f = lb + (1.0 - lb) * _sigmoid(ps_ref[slot, 1, rows, :])
        b = functools.reduce(lambda x, y: x + y,
                             [jnp.dot(tril, part, preferred_element_type=f32) for part in _split2(jnp.log2(f))])
        return slot, rows, 1.0 - f, b

    def operands(slot, rows, k, b):
        q = ps_ref[slot, 0, rows, :]
        qd, kd, q0, decay, us = [], [], [], [], []
        for ci in range(nck):
            sl = slice(ci * c, (ci + 1) * c)
            bc = b[sl, :]
            b_mid = bc[half - 1:half, :]
            b_last = bc[c - 1:c, :]
            qm = q[sl, :] * jnp.exp2(bc - b_mid)
            qd.append(qm.astype(bf16))
            kd.append((k[sl, :] * jnp.exp2(b_mid - bc)).astype(bf16))
            q0.append((qm * jnp.exp2(b_mid)).astype(bf16))
            decay.append(jnp.exp2(b_last))
            us.append(jnp.exp2(b_last - b_mid))
        return slot, rows, qd, kd, q0, decay, us, ps_ref[slot, 2, rows, :].astype(bf16)

    def intra_chunk(slot, rows, qd, kd, q0, decay, us, v):
        qd_all = jnp.concatenate(qd, axis=0)
        kd_all = jnp.concatenate(kd, axis=0)
        intra = []
        for h in range(2):
            ln = slice(h * HEAD, (h + 1) * HEAD)
            s = lax.dot_general(qd_all[:, ln], kd_all[:, ln], nt, preferred_element_type=f32)
            s = jnp.where(causal, s, 0.0).astype(bf16)
            intra.append(jnp.dot(s, v[:, ln], preferred_element_type=f32))
        upd = [[lax.dot_general(v[ci * c:(ci + 1) * c, h * HEAD:(h + 1) * HEAD],
                                kd[ci][:, h * HEAD:(h + 1) * HEAD], tn, preferred_element_type=f32)
                * us[ci][:, h * HEAD:(h + 1) * HEAD] for h in range(2)] for ci in range(nck)]
        return slot, rows, jnp.concatenate(intra, axis=1), upd, q0, decay

    def recur(out_row0, st, slot, rows, o_intra, upd, q0, decay):
        gate = ps_ref[slot, 3, rows, :]
        for ci in range(nck):
            sl = slice(ci * c, (ci + 1) * c)
            s0 = st[0].astype(bf16)
            s1 = st[1].astype(bf16)
            both = jnp.concatenate([jnp.concatenate([s0, zero_blk], axis=1),
                                    jnp.concatenate([zero_blk, s1], axis=1)], axis=0)
            o = o_intra[sl, :] + lax.dot_general(q0[ci], both, nt, preferred_element_type=f32)
            for h in range(2):
                ln = slice(h * HEAD, (h + 1) * HEAD)
                st[h] = st[h] * decay[ci][:, ln] + upd[ci][h]
                oh = o[:, ln]
                oh = oh * lax.rsqrt(jnp.mean(oh * oh, axis=-1, keepdims=True) + EPS)
                gh = gate[sl, ln]
                r0 = out_row0 + rows.start + ci * c
                o_ref[r0:r0 + c, ln] = (oh * ng[:, ln] * (gh * _sigmoid(gh))).astype(o_ref.dtype)
        return st

    def mix(slot, out_row0, st):
        stage = [decays(slot, g) for g in range(HG_BLOCK // rg)]
        stage = [operands(*x) for x in stage]
        stage = [intra_chunk(*x) for x in stage]
        for x in stage:
            st = recur(out_row0, st, *x)
        return st

    st = [st_ref[0], st_ref[1]]
    n_blocks = HG_TILE // HG_BLOCK
    for i in range(n_blocks):
        if i + 1 < n_blocks:
            project(u_ref, (i + 1) * HG_BLOCK, (i + 1) % 2)
        else:
            project(un_ref, 0, 0)
        st = mix(i % 2, i * HG_BLOCK, st)
    st_ref[0] = st[0]
    st_ref[1] = st[1]


def _hgrn(u, w_in, lb, ng, batch, seq):
    n, dm = u.shape
    w = w_in.shape[1] // 4
    pairs = w // (2 * HEAD)
    nt = seq // HG_TILE
    blk = (HG_TILE, 2 * HEAD)

    def wspec(k):
        return pl.BlockSpec((dm, 2 * HEAD), lambda b, pr, t, k=k: (0, k * pairs + pr))

    vec = pl.BlockSpec((1, 2 * HEAD), lambda b, pr, t: (0, pr))
    return pl.pallas_call(
        _hgrn_kernel,
        out_shape=jax.ShapeDtypeStruct((n, w), bf16),
        grid=(batch, pairs, nt),
        in_specs=[pl.BlockSpec((HG_TILE, dm), lambda b, pr, t: (b * nt + t, 0)),
                  pl.BlockSpec((HG_TILE, dm), lambda b, pr, t: (b * nt + jnp.minimum(t + 1, nt - 1), 0)),
                  wspec(0), wspec(1), wspec(2), wspec(3), vec, vec],
        out_specs=pl.BlockSpec(blk, lambda b, pr, t: (b * nt + t, pr)),
        scratch_shapes=[pltpu.VMEM((2, HEAD, HEAD), f32),
                        pltpu.VMEM((2, 4, HG_BLOCK, 2 * HEAD), f32),
                        pltpu.VMEM((4, dm, 2 * HEAD), bf16)],
        compiler_params=_cparams(("parallel", "parallel", "arbitrary")),
        name="hgrn2",
    )(u, u, w_in, w_in, w_in, w_in, lb.reshape(1, w), ng.reshape(1, w))


MERGE_D = 4
LOG2E = math.log2(math.e)


def _attn_kernel(u_ref, wq_ref, wk_ref, wv_ref, wg_ref, q1_ref, k1_ref, v1_ref, q2_ref, k2_ref, v2_ref,
                 bias_ref, o_ref, ck0, cv0, ck1, cv1, ck2, cv2, oacc, lacc, macc, nat, p0, w0):
    t = pl.program_id(2)
    par = t % 2
    q_refs = (p0.at[0], q1_ref, q2_ref)
    k_refs = (p0.at[1], k1_ref, k2_ref)
    v_refs = (p0.at[2], v1_ref, v2_ref)
    gate_ref = p0.at[3]
    cks = (ck0, ck1, ck2)
    cvs = (cv0, cv1, cv2)
    scale2 = HEAD ** -0.5 * LOG2E
    blk = ATT_BLOCK
    res_rows = ATT_TILE // MERGE_D

    @pl.when(t == 0)
    def _():
        for g, (_, d) in enumerate(DILATED_GROUPS):
            cks[g][:, 2 * blk:, :] = jnp.zeros((d, blk, HEAD), bf16)
            cvs[g][:, 2 * blk:, :] = jnp.zeros((d, blk, HEAD), bf16)
        for k, w_ref in enumerate((wq_ref, wk_ref, wv_ref, wg_ref)):
            w0[:, k * HEAD:(k + 1) * HEAD] = w_ref[...].astype(bf16)

    proj_rows = ROW_CHUNK

    def project(c0):
        y = jnp.dot(u_ref[c0:c0 + proj_rows, :], w0[...], preferred_element_type=f32).astype(bf16)
        for k in range(4):
            p0[k, c0:c0 + proj_rows, :] = y[:, k * HEAD:(k + 1) * HEAD]

    slot = pl.ds(pl.multiple_of(par * 2 * blk, blk), blk)

    def stage_carry(g):
        d = DILATED_GROUPS[g][1]
        per_res = ATT_TILE // d
        for r in range(d):
            for src, dst in ((k_refs[g], cks[g]), (v_refs[g], cvs[g])):
                dst[r, blk:2 * blk, :] = src[r * per_res:r * per_res + blk, :]
                dst[r, slot, :] = src[(r + 1) * per_res - blk:(r + 1) * per_res, :]

    cwin = pl.ds(pl.multiple_of((1 - par) * blk, blk), 2 * blk)
    colid = lax.broadcasted_iota(jnp.int32, (blk, 2 * blk), 1)
    pen = jnp.where(colid >= blk, jnp.where(t == 0, NEG, 0.0).astype(f32), 0.0)

    def carry_bias(g):
        b = bias_ref[g]
        swapped = jnp.concatenate([b[:, blk:], b[:, :blk]], axis=1)
        return jnp.where(par == 1, b, swapped) + pen

    def attend(q, kwin, vwin, bias):
        s = lax.dot_general(q, kwin, (((1,), (1,)), ((), ())), preferred_element_type=f32)
        s = s * scale2 + bias
        m = jnp.max(s, axis=-1, keepdims=True)
        p = jnp.exp2(s - m)
        l = jnp.sum(p, axis=-1, keepdims=True)
        acc = jnp.dot(p.astype(bf16), vwin, preferred_element_type=f32)
        return acc, jnp.broadcast_to(l, (blk, HEAD)), jnp.broadcast_to(m, (blk, HEAD))

    def block(g, r, j):
        per_res = ATT_TILE // DILATED_GROUPS[g][1]
        q = q_refs[g][r * per_res + j * blk:r * per_res + (j + 1) * blk, :]
        if j == 0:
            return attend(q, cks[g][r, cwin, :], cvs[g][r, cwin, :], carry_bias(g))
        win = slice(r * per_res + (j - 1) * blk, r * per_res + (j + 1) * blk)
        return attend(q, k_refs[g][win, :], v_refs[g][win, :], bias_ref[g])

    dsts = (oacc, lacc, macc)

    def group1(r, j):
        row0 = r * (ATT_TILE // d1) + j * blk
        for k, val in enumerate(block(1, r, j)):
            dsts[k][1, row0:row0 + blk, :] = val

    def group2(r16):
        rows = pl.ds((r16 % MERGE_D) * res_rows + r16 // MERGE_D, blk, stride=d2 // MERGE_D)
        for k, val in enumerate(block(2, r16, 0)):
            dsts[k][2, rows, :] = val

    def group0(j):
        for k, val in enumerate(block(0, 0, j)):
            dsts[k][0, j * blk:(j + 1) * blk, :] = val

    d1 = DILATED_GROUPS[1][1]
    d2 = DILATED_GROUPS[2][1]
    stage_carry(1)
    stage_carry(2)
    dilated = ([functools.partial(group1, r, j) for r in range(d1) for j in range(ATT_TILE // d1 // blk)]
               + [functools.partial(group2, r16) for r16 in range(d2)])
    n_proj = ATT_TILE // proj_rows
    per_proj = len(dilated) // n_proj
    for i in range(n_proj):
        project(i * proj_rows)
        for task in dilated[i * per_proj:(i + 1) * per_proj]:
            task()
    stage_carry(0)
    for j in range(ATT_TILE // blk):
        group0(j)

    def merge(c, carry):
        for r4 in range(MERGE_D):
            rows = pl.ds(pl.multiple_of(r4 * res_rows + c * blk, blk), blk)
            toks = pl.ds(c * blk * MERGE_D + r4, blk, stride=MERGE_D)
            at = (toks, rows, rows)
            ms = [macc[g, at[g], :] for g in range(N_GROUPS)]
            mx = functools.reduce(jnp.maximum, ms)
            ws = [jnp.exp2(x - mx) for x in ms]
            num = functools.reduce(lambda a, b: a + b, [w * oacc[g, at[g], :] for g, w in enumerate(ws)])
            den = functools.reduce(lambda a, b: a + b, [w * lacc[g, at[g], :] for g, w in enumerate(ws)])
            nat[toks, :] = num / den
        return carry

    lax.fori_loop(0, res_rows // blk, merge, 0)

    rc = 256
    for c0 in range(0, ATT_TILE, rc):
        rows = slice(c0, c0 + rc)
        gate = gate_ref[rows, :].astype(f32)
        o_ref[rows, :] = (nat[rows, :] * (gate * _sigmoid(gate))).astype(o_ref.dtype)


def _attention(u, w_in, p, bias, batch, seq, heads):
    n, dm = u.shape
    nt = seq // ATT_TILE
    blk = (ATT_TILE, HEAD)

    def spec(k):
        return pl.BlockSpec(blk, lambda b, h, t, k=k: (b * nt + t, k * heads + h))

    def wspec(k):
        return pl.BlockSpec((dm, HEAD), lambda b, h, t, k=k: (0, k * heads + h))

    scratch = []
    for _, d in DILATED_GROUPS:
        scratch += [pltpu.VMEM((d, 3 * ATT_BLOCK, HEAD), bf16)] * 2
    scratch += [pltpu.VMEM((N_GROUPS, ATT_TILE, HEAD), f32)] * 3
    scratch += [pltpu.VMEM((ATT_TILE, HEAD), f32)]
    scratch += [pltpu.VMEM((4, ATT_TILE, HEAD), bf16), pltpu.VMEM((dm, 4 * HEAD), bf16)]
    return pl.pallas_call(
        _attn_kernel,
        out_shape=jax.ShapeDtypeStruct((n, heads * HEAD), bf16),
        grid=(batch, heads, nt),
        in_specs=[pl.BlockSpec((ATT_TILE, dm), lambda b, h, t: (b * nt + t, 0)),
                  wspec(0), wspec(1), wspec(2), wspec(3 * N_GROUPS)] + [spec(k) for k in range(6)] + [
                  pl.BlockSpec((N_GROUPS, None, ATT_BLOCK, 2 * ATT_BLOCK), lambda b, h, t: (0, h, 0, 0))],
        out_specs=pl.BlockSpec(blk, lambda b, h, t: (b * nt + t, h)),
        scratch_shapes=scratch,
        compiler_params=_cparams(("parallel", "parallel", "arbitrary")),
        name="dilated_attention",
    )(u, w_in, w_in, w_in, w_in, *([p] * 6), bias)


def _t5_bucket(dist):
    max_exact = N_BUCKETS // 2
    df = jnp.maximum(dist, 1).astype(f32)
    large = max_exact + (jnp.log(df / max_exact) / math.log(MAX_DISTANCE / max_exact)
                         * (N_BUCKETS - max_exact)).astype(jnp.int32)
    large = jnp.minimum(large, N_BUCKETS - 1)
    return jnp.where(dist < max_exact, dist, large)


def _bias_tables(rel_bias, heads):
    a = jnp.arange(ATT_BLOCK)[:, None]
    c = jnp.arange(2 * ATT_BLOCK)[None, :]
    rel = ATT_BLOCK + a - c
    tables = []
    for g, (window, d) in enumerate(DILATED_GROUPS):
        assert window // d == ATT_BLOCK and ATT_TILE % (d * ATT_BLOCK) == 0
        band = (rel >= 0) & (rel <= window // d)
        onehot = jax.nn.one_hot(_t5_bucket(jnp.maximum(rel, 0) * d), N_BUCKETS, dtype=f32)
        tab = jnp.einsum("acb,bh->hac", onehot, rel_bias[:, g * heads:(g + 1) * heads].astype(f32),
                         precision=lax.Precision.HIGHEST)
        tables.append(jnp.where(band[None], tab * LOG2E, NEG))
    return jnp.stack(tables, axis=0)


PERM_ROWS = 256


def _groupproj_kernel(a_ref, w_ref, o_ref, perm_ref, *, col_blocks_per_group):
    j = pl.program_id(1)
    tm = a_ref.shape[0]

    @pl.when(j == 0)
    def _():
        i = lax.broadcasted_iota(jnp.int32, (PERM_ROWS, PERM_ROWS), 0)
        k = lax.broadcasted_iota(jnp.int32, (PERM_ROWS, PERM_ROWS), 1)
        for g, (_, d) in enumerate(DILATED_GROUPS[1:]):
            per = PERM_ROWS // d
            perm = jnp.where(k == (i % per) * d + i // per, 1.0, 0.0).astype(bf16)
            for sb in range(tm // PERM_ROWS):
                y = jnp.dot(perm, a_ref[sb * PERM_ROWS:(sb + 1) * PERM_ROWS, :],
                            preferred_element_type=f32).astype(bf16)
                for r in range(d):
                    dst = r * (tm // d) + sb * per
                    perm_ref[g, dst:dst + per, :] = y[r * per:(r + 1) * per, :]

    def project(lhs_ref):
        w = w_ref[...].astype(bf16)
        rc = ROW_CHUNK
        for r in range(tm // rc):
            rows = slice(r * rc, (r + 1) * rc)
            o_ref[rows, :] = jnp.dot(lhs_ref[rows, :], w, preferred_element_type=f32).astype(o_ref.dtype)

    grp = j // col_blocks_per_group
    for g in range(N_GROUPS - 1):
        pl.when(grp == g)(functools.partial(project, perm_ref.at[g]))


def _groupproj(u, w, width, tn=2048):
    n, k = u.shape
    c = 3 * (N_GROUPS - 1) * width
    tn = math.gcd(tn, width)
    first = 3 * width // tn
    return pl.pallas_call(
        functools.partial(_groupproj_kernel, col_blocks_per_group=3 * width // tn),
        out_shape=jax.ShapeDtypeStruct((n, c), bf16),
        grid=(n // ATT_TILE, c // tn),
        in_specs=[pl.BlockSpec((ATT_TILE, k), lambda i, j: (i, 0)),
                  pl.BlockSpec((k, tn), lambda i, j: (0, j + first))],
        out_specs=pl.BlockSpec((ATT_TILE, tn), lambda i, j: (i, j)),
        scratch_shapes=[pltpu.VMEM((N_GROUPS - 1, ATT_TILE, k), bf16)],
        compiler_params=_cparams(("parallel", "arbitrary")),
        name="group_proj",
    )(u, w)


def kernel(x, ln_g, hg_w_in, hg_lb_logits, hg_norm_g, hg_w_out, att_w_in, att_w_out, rel_bias, final_g):
    batch, seq, d_model = x.shape
    n = batch * seq
    w = hg_w_out.shape[1]
    heads = w // HEAD
    assert seq % ATT_TILE == 0 and seq % HG_TILE == 0 and w % (2 * HEAD) == 0

    lower = jnp.cumsum(jax.nn.softmax(hg_lb_logits.astype(f32), axis=0), axis=0)
    h0 = x.reshape(n, d_model)

    u0 = _rmsnorm(h0, ln_g[0], bf16)
    y0 = _hgrn(u0, hg_w_in[0], lower[0], hg_norm_g[0], batch, seq)
    h1, u1 = _outproj(y0, hg_w_out[0], h0, ln_g[1], bf16, emit_h=True)

    wa = att_w_in[0]
    y1 = _attention(u1, wa, _groupproj(u1, wa, w), _bias_tables(rel_bias, heads), batch, seq, heads)
    (out,) = _outproj(y1, att_w_out[0], h1, final_g, f32, emit_h=False)
    return out.reshape(batch, seq, d_model)
```

```python
import functools
import math

import jax
import jax.numpy as jnp
from jax import lax
from jax.experimental import pallas as pl
from jax.experimental.pallas import tpu as pltpu

EPS = 1e-6
HEAD = 128
HG_CHUNK = 64
ATT_BLOCK = 128
DILATED_GROUPS = ((128, 1), (512, 4), (2048, 16))
N_GROUPS = len(DILATED_GROUPS)
N_BUCKETS = 32
MAX_DISTANCE = 2048
ATT_TILE = ATT_BLOCK * max(d for _, d in DILATED_GROUPS)
NEG = -1e30

VMEM_LIMIT = 56 * 1024 * 1024
ROW_CHUNK = 512

f32 = jnp.float32
bf16 = jnp.bfloat16


def _cparams(sem):
    return pltpu.CompilerParams(dimension_semantics=sem, vmem_limit_bytes=VMEM_LIMIT)


def _rmsnorm_kernel(x_ref, g_ref, o_ref):
    x = x_ref[...]
    ms = jnp.mean(x * x, axis=-1, keepdims=True)
    o_ref[...] = (x * lax.rsqrt(ms + EPS) * g_ref[...]).astype(o_ref.dtype)


def _rmsnorm(x, g, out_dtype, tm=1024):
    n, d = x.shape
    return pl.pallas_call(
        _rmsnorm_kernel,
        out_shape=jax.ShapeDtypeStruct((n, d), out_dtype),
        grid=(n // tm,),
        in_specs=[pl.BlockSpec((tm, d), lambda i: (i, 0)),
                  pl.BlockSpec((1, d), lambda i: (0, 0))],
        out_specs=pl.BlockSpec((tm, d), lambda i: (i, 0)),
        compiler_params=_cparams(("parallel",)),
        name="rmsnorm",
    )(x, g.reshape(1, d))


def _outproj_kernel(y_ref, w_ref, h_ref, g_ref, *refs, emit_h):
    *out_refs, wb_ref = refs

    @pl.when(pl.program_id(0) == 0)
    def _():
        wb_ref[...] = w_ref[...].astype(bf16)

    h = h_ref[...] + jnp.dot(y_ref[...], wb_ref[...], preferred_element_type=f32)
    if emit_h:
        out_refs[0][...] = h
    n_ref = out_refs[-1]
    ms = jnp.mean(h * h, axis=-1, keepdims=True)
    n_ref[...] = (h * lax.rsqrt(ms + EPS) * g_ref[...]).astype(n_ref.dtype)


def _outproj(y, w, h, g, norm_dtype, emit_h, tm=1024):
    n, k = y.shape
    d = w.shape[1]
    tm = min(tm, n)
    row = lambda i: (i, 0)
    out_shape = [jax.ShapeDtypeStruct((n, d), norm_dtype)]
    out_specs = [pl.BlockSpec((tm, d), row)]
    if emit_h:
        out_shape.insert(0, jax.ShapeDtypeStruct((n, d), f32))
        out_specs.insert(0, pl.BlockSpec((tm, d), row))
    return pl.pallas_call(
        functools.partial(_outproj_kernel, emit_h=emit_h),
        out_shape=out_shape,
        grid=(n // tm,),
        in_specs=[pl.BlockSpec((tm, k), row),
                  pl.BlockSpec((k, d), lambda i: (0, 0)),
                  pl.BlockSpec((tm, d), row),
                  pl.BlockSpec((1, d), lambda i: (0, 0))],
        out_specs=out_specs,
        scratch_shapes=[pltpu.VMEM((k, d), bf16)],
        compiler_params=_cparams(("arbitrary",)),
        name="outproj",
    )(y, w, h, g.reshape(1, d))


HG_GROUP = 4 * HG_CHUNK
HG_BLOCK = 2 * HG_GROUP
HG_TILE = 4 * HG_BLOCK


def _sigmoid(x):
    return 1.0 / (1.0 + jnp.exp(-x))


def _split2(x):
    hi = x.astype(bf16)
    return hi, (x - hi.astype(f32)).astype(bf16)


def _hgrn_kernel(u_ref, un_ref, wq_ref, wf_ref, wi_ref, wg_ref, lb_ref, ng_ref, o_ref, st_ref, ps_ref, wb_ref):
    c = HG_CHUNK
    half = c // 2
    rg = HG_GROUP
    nck = rg // c
    w_refs = (wq_ref, wf_ref, wi_ref, wg_ref)

    def project(src_ref, row0, slot):
        u = src_ref[row0:row0 + HG_BLOCK, :]
        for k in range(len(w_refs)):
            ps_ref[slot, k] = jnp.dot(u, wb_ref[k], preferred_element_type=f32)

    @pl.when(pl.program_id(2) == 0)
    def _():
        st_ref[...] = jnp.zeros_like(st_ref)
        for k, w_ref in enumerate(w_refs):
            wb_ref[k] = w_ref[...].astype(bf16)
        project(u_ref, 0, 0)

    lb = lb_ref[...]
    ng = ng_ref[...]
    row = lax.broadcasted_iota(jnp.int32, (rg, rg), 0)
    col = lax.broadcasted_iota(jnp.int32, (rg, rg), 1)
    causal = (row >= col) & ((row // c) == (col // c))
    tril = jnp.where(causal, 1.0, 0.0).astype(bf16)
    zero_blk = jnp.zeros((HEAD, HEAD), bf16)
    nt = (((1,), (1,)), ((), ()))
    tn = (((0,), (0,)), ((), ()))

    def decays(slot, g):
        rows = slice(g * rg, (g + 1) * rg)
        f = lb + (1.0 - lb) * _sigmoid(ps_ref[slot, 1, rows, :])
        b = functools.reduce(lambda x, y: x + y,
                             [jnp.dot(tril, part, preferred_element_type=f32) for part in _split2(jnp.log2(f))])
        return slot, rows, 1.0 - f, b

    def operands(slot, rows, k, b):
        q = ps_ref[slot, 0, rows, :]
        qd, kd, q0, decay, us = [], [], [], [], []
        for ci in range(nck):
            sl = slice(ci * c, (ci + 1) * c)
            bc = b[sl, :]
            b_mid = bc[half - 1:half, :]
            b_last = bc[c - 1:c, :]
            qm = q[sl, :] * jnp.exp2(bc - b_mid)
            qd.append(qm.astype(bf16))
            kd.append((k[sl, :] * jnp.exp2(b_mid - bc)).astype(bf16))
            q0.append((qm * jnp.exp2(b_mid)).astype(bf16))
            decay.append(jnp.exp2(b_last))
            us.append(jnp.exp2(b_last - b_mid))
        return slot, rows, qd, kd, q0, decay, us, ps_ref[slot, 2, rows, :].astype(bf16)

    def intra_chunk(slot, rows, qd, kd, q0, decay, us, v):
        qd_all = jnp.concatenate(qd, axis=0)
        kd_all = jnp.concatenate(kd, axis=0)
        intra = []
        for h in range(2):
            ln = slice(h * HEAD, (h + 1) * HEAD)
            s = lax.dot_general(qd_all[:, ln], kd_all[:, ln], nt, preferred_element_type=f32)
            s = jnp.where(causal, s, 0.0).astype(bf16)
            intra.append(jnp.dot(s, v[:, ln], preferred_element_type=f32))
        upd = [[lax.dot_general(v[ci * c:(ci + 1) * c, h * HEAD:(h + 1) * HEAD],
                                kd[ci][:, h * HEAD:(h + 1) * HEAD], tn, preferred_element_type=f32)
                * us[ci][:, h * HEAD:(h + 1) * HEAD] for h in range(2)] for ci in range(nck)]
        return slot, rows, jnp.concatenate(intra, axis=1), upd, q0, decay

    def recur(out_row0, st, slot, rows, o_intra, upd, q0, decay):
        gate = ps_ref[slot, 3, rows, :]
        for ci in range(nck):
            sl = slice(ci * c, (ci + 1) * c)
            s0 = st[0].astype(bf16)
            s1 = st[1].astype(bf16)
            both = jnp.concatenate([jnp.concatenate([s0, zero_blk], axis=1),
                                    jnp.concatenate([zero_blk, s1], axis=1)], axis=0)
            o = o_intra[sl, :] + lax.dot_general(q0[ci], both, nt, preferred_element_type=f32)
            for h in range(2):
                ln = slice(h * HEAD, (h + 1) * HEAD)
                st[h] = st[h] * decay[ci][:, ln] + upd[ci][h]
                oh = o[:, ln]
                oh = oh * lax.rsqrt(jnp.mean(oh * oh, axis=-1, keepdims=True) + EPS)
                gh = gate[sl, ln]
                r0 = out_row0 + rows.start + ci * c
                o_ref[r0:r0 + c, ln] = (oh * ng[:, ln] * (gh * _sigmoid(gh))).astype(o_ref.dtype)
        return st

    def mix(slot, out_row0, st):
        stage = [decays(slot, g) for g in range(HG_BLOCK // rg)]
        stage = [operands(*x) for x in stage]
        stage = [intra_chunk(*x) for x in stage]
        for x in stage:
            st = recur(out_row0, st, *x)
        return st

    st = [st_ref[0], st_ref[1]]
    n_blocks = HG_TILE // HG_BLOCK
    for i in range(n_blocks):
        if i + 1 < n_blocks:
            project(u_ref, (i + 1) * HG_BLOCK, (i + 1) % 2)
        else:
            project(un_ref, 0, 0)
        st = mix(i % 2, i * HG_BLOCK, st)
    st_ref[0] = st[0]
    st_ref[1] = st[1]


def _hgrn(u, w_in, lb, ng, batch, seq):
    n, dm = u.shape
    w = w_in.shape[1] // 4
    pairs = w // (2 * HEAD)
    nt = seq // HG_TILE
    blk = (HG_TILE, 2 * HEAD)

    def wspec(k):
        return pl.BlockSpec((dm, 2 * HEAD), lambda b, pr, t, k=k: (0, k * pairs + pr))

    vec = pl.BlockSpec((1, 2 * HEAD), lambda b, pr, t: (0, pr))
    return pl.pallas_call(
        _hgrn_kernel,
        out_shape=jax.ShapeDtypeStruct((n, w), bf16),
        grid=(batch, pairs, nt),
        in_specs=[pl.BlockSpec((HG_TILE, dm), lambda b, pr, t: (b * nt + t, 0)),
                  pl.BlockSpec((HG_TILE, dm), lambda b, pr, t: (b * nt + jnp.minimum(t + 1, nt - 1), 0)),
                  wspec(0), wspec(1), wspec(2), wspec(3), vec, vec],
        out_specs=pl.BlockSpec(blk, lambda b, pr, t: (b * nt + t, pr)),
        scratch_shapes=[pltpu.VMEM((2, HEAD, HEAD), f32),
                        pltpu.VMEM((2, 4, HG_BLOCK, 2 * HEAD), f32),
                        pltpu.VMEM((4, dm, 2 * HEAD), bf16)],
        compiler_params=_cparams(("parallel", "parallel", "arbitrary")),
        name="hgrn2",
    )(u, u, w_in, w_in, w_in, w_in, lb.reshape(1, w), ng.reshape(1, w))


MERGE_D = 4
LOG2E = math.log2(math.e)
QK_SCALE = HEAD ** -0.5 * LOG2E


def _attn_kernel(u_ref, wq_ref, wk_ref, wv_ref, wg_ref, q1_ref, k1_ref, v1_ref, q2_ref, k2_ref, v2_ref,
                 bias_ref, o_ref, ck0, cv0, ck1, cv1, ck2, cv2, oacc, lacc, macc, nat, p0, w0):
    t = pl.program_id(2)
    par = t % 2
    q_refs = (p0.at[0], q1_ref, q2_ref)
    k_refs = (p0.at[1], k1_ref, k2_ref)
    v_refs = (p0.at[2], v1_ref, v2_ref)
    gate_ref = p0.at[3]
    cks = (ck0, ck1, ck2)
    cvs = (cv0, cv1, cv2)
    blk = ATT_BLOCK
    res_rows = ATT_TILE // MERGE_D

    @pl.when(t == 0)
    def _():
        for g, (_, d) in enumerate(DILATED_GROUPS):
            cks[g][:, 2 * blk:, :] = jnp.zeros((d, blk, HEAD), bf16)
            cvs[g][:, 2 * blk:, :] = jnp.zeros((d, blk, HEAD), bf16)
        for k, w_ref in enumerate((wq_ref, wk_ref, wv_ref, wg_ref)):
            w0[:, k * HEAD:(k + 1) * HEAD] = (w_ref[...] * (QK_SCALE if k == 0 else 1.0)).astype(bf16)

    proj_rows = ROW_CHUNK // 2

    def project(c0):
        y = jnp.dot(u_ref[c0:c0 + proj_rows, :], w0[...], preferred_element_type=f32).astype(bf16)
        for k in range(4):
            p0[k, c0:c0 + proj_rows, :] = y[:, k * HEAD:(k + 1) * HEAD]

    slot = pl.ds(pl.multiple_of(par * 2 * blk, blk), blk)

    def stage_carry(g):
        d = DILATED_GROUPS[g][1]
        per_res = ATT_TILE // d
        for r in range(d):
            for src, dst in ((k_refs[g], cks[g]), (v_refs[g], cvs[g])):
                dst[r, blk:2 * blk, :] = src[r * per_res:r * per_res + blk, :]
                dst[r, slot, :] = src[(r + 1) * per_res - blk:(r + 1) * per_res, :]

    cwin = pl.ds(pl.multiple_of((1 - par) * blk, blk), 2 * blk)
    colid = lax.broadcasted_iota(jnp.int32, (blk, 2 * blk), 1)
    pen = jnp.where(colid >= blk, jnp.where(t == 0, NEG, 0.0).astype(f32), 0.0)

    def carry_bias(g):
        b = bias_ref[g]
        swapped = jnp.concatenate([b[:, blk:], b[:, :blk]], axis=1)
        return jnp.where(par == 1, b, swapped) + pen

    carry_biases = [carry_bias(g) for g in range(N_GROUPS)]

    def attend(q, kwin, vwin, bias):
        s = lax.dot_general(q, kwin, (((1,), (1,)), ((), ())), preferred_element_type=f32)
        s = s + bias
        m = jnp.max(s, axis=-1, keepdims=True)
        p = jnp.exp2(s - m)
        l = jnp.sum(p, axis=-1, keepdims=True)
        acc = jnp.dot(p.astype(bf16), vwin, preferred_element_type=f32)
        return acc, jnp.broadcast_to(l, (blk, HEAD)), jnp.broadcast_to(m, (blk, HEAD))

    def block(g, r, j):
        per_res = ATT_TILE // DILATED_GROUPS[g][1]
        q = q_refs[g][r * per_res + j * blk:r * per_res + (j + 1) * blk, :]
        if j == 0:
            return attend(q, cks[g][r, cwin, :], cvs[g][r, cwin, :], carry_biases[g])
        win = slice(r * per_res + (j - 1) * blk, r * per_res + (j + 1) * blk)
        return attend(q, k_refs[g][win, :], v_refs[g][win, :], bias_ref[g])

    dsts = (oacc, lacc, macc)

    def group1(r, j):
        row0 = r * (ATT_TILE // d1) + j * blk
        for k, val in enumerate(block(1, r, j)):
            dsts[k][1, row0:row0 + blk, :] = val

    def group2(r16):
        rows = pl.ds((r16 % MERGE_D) * res_rows + r16 // MERGE_D, blk, stride=d2 // MERGE_D)
        for k, val in enumerate(block(2, r16, 0)):
            dsts[k][2, rows, :] = val

    def group0(j):
        for k, val in enumerate(block(0, 0, j)):
            dsts[k][0, j * blk:(j + 1) * blk, :] = val

    d1 = DILATED_GROUPS[1][1]
    d2 = DILATED_GROUPS[2][1]
    stage_carry(1)
    stage_carry(2)
    dilated = ([functools.partial(group1, r, j) for r in range(d1) for j in range(ATT_TILE // d1 // blk)]
               + [functools.partial(group2, r16) for r16 in range(d2)])
    n_proj = ATT_TILE // proj_rows
    per_proj = len(dilated) // n_proj
    for i in range(n_proj):
        project(i * proj_rows)
        for task in dilated[i * per_proj:(i + 1) * per_proj]:
            task()
    stage_carry(0)
    for j in range(ATT_TILE // blk):
        group0(j)

    def merge(c, carry):
        for r4 in range(MERGE_D):
            rows = pl.ds(pl.multiple_of(r4 * res_rows + c * blk, blk), blk)
            toks = pl.ds(c * blk * MERGE_D + r4, blk, stride=MERGE_D)
            at = (toks, rows, rows)
            ms = [macc[g, at[g], :] for g in range(N_GROUPS)]
            mx = functools.reduce(jnp.maximum, ms)
            ws = [jnp.exp2(x - mx) for x in ms]
            num = functools.reduce(lambda a, b: a + b, [w * oacc[g, at[g], :] for g, w in enumerate(ws)])
            den = functools.reduce(lambda a, b: a + b, [w * lacc[g, at[g], :] for g, w in enumerate(ws)])
            nat[toks, :] = num / den
        return carry

    lax.fori_loop(0, res_rows // blk, merge, 0)

    rc = 256
    for c0 in range(0, ATT_TILE, rc):
        rows = slice(c0, c0 + rc)
        gate = gate_ref[rows, :].astype(f32)
        o_ref[rows, :] = (nat[rows, :] * (gate * _sigmoid(gate))).astype(o_ref.dtype)


def _attention(u, w_in, p, bias, batch, seq, heads):
    n, dm = u.shape
    nt = seq // ATT_TILE
    blk = (ATT_TILE, HEAD)

    def spec(k):
        return pl.BlockSpec(blk, lambda b, h, t, k=k: (b * nt + t, k * heads + h))

    def wspec(k):
        return pl.BlockSpec((dm, HEAD), lambda b, h, t, k=k: (0, k * heads + h))

    scratch = []
    for _, d in DILATED_GROUPS:
        scratch += [pltpu.VMEM((d, 3 * ATT_BLOCK, HEAD), bf16)] * 2
    scratch += [pltpu.VMEM((N_GROUPS, ATT_TILE, HEAD), f32)] * 3
    scratch += [pltpu.VMEM((ATT_TILE, HEAD), f32)]
    scratch += [pltpu.VMEM((4, ATT_TILE, HEAD), bf16), pltpu.VMEM((dm, 4 * HEAD), bf16)]
    return pl.pallas_call(
        _attn_kernel,
        out_shape=jax.ShapeDtypeStruct((n, heads * HEAD), bf16),
        grid=(batch, heads, nt),
        in_specs=[pl.BlockSpec((ATT_TILE, dm), lambda b, h, t: (b * nt + t, 0)),
                  wspec(0), wspec(1), wspec(2), wspec(3 * N_GROUPS)] + [spec(k) for k in range(6)] + [
                  pl.BlockSpec((N_GROUPS, None, ATT_BLOCK, 2 * ATT_BLOCK), lambda b, h, t: (0, h, 0, 0))],
        out_specs=pl.BlockSpec(blk, lambda b, h, t: (b * nt + t, h)),
        scratch_shapes=scratch,
        compiler_params=_cparams(("parallel", "parallel", "arbitrary")),
        name="dilated_attention",
    )(u, w_in, w_in, w_in, w_in, *([p] * 6), bias)


def _t5_bucket(dist):
    max_exact = N_BUCKETS // 2
    df = jnp.maximum(dist, 1).astype(f32)
    large = max_exact + (jnp.log(df / max_exact) / math.log(MAX_DISTANCE / max_exact)
                         * (N_BUCKETS - max_exact)).astype(jnp.int32)
    large = jnp.minimum(large, N_BUCKETS - 1)
    return jnp.where(dist < max_exact, dist, large)


def _bias_tables(rel_bias, heads):
    a = jnp.arange(ATT_BLOCK)[:, None]
    c = jnp.arange(2 * ATT_BLOCK)[None, :]
    rel = ATT_BLOCK + a - c
    tables = []
    for g, (window, d) in enumerate(DILATED_GROUPS):
        assert window // d == ATT_BLOCK and ATT_TILE % (d * ATT_BLOCK) == 0
        band = (rel >= 0) & (rel <= window // d)
        onehot = jax.nn.one_hot(_t5_bucket(jnp.maximum(rel, 0) * d), N_BUCKETS, dtype=f32)
        tab = jnp.einsum("acb,bh->hac", onehot, rel_bias[:, g * heads:(g + 1) * heads].astype(f32),
                         precision=lax.Precision.HIGHEST)
        tables.append(jnp.where(band[None], tab * LOG2E, NEG))
    return jnp.stack(tables, axis=0)


PERM_ROWS = 256


def _groupproj_kernel(a_ref, w_ref, o_ref, perm_ref, *, col_blocks_per_group):
    j = pl.program_id(1)
    tm = a_ref.shape[0]

    @pl.when(j == 0)
    def _():
        i = lax.broadcasted_iota(jnp.int32, (PERM_ROWS, PERM_ROWS), 0)
        k = lax.broadcasted_iota(jnp.int32, (PERM_ROWS, PERM_ROWS), 1)
        for g, (_, d) in enumerate(DILATED_GROUPS[1:]):
            per = PERM_ROWS // d
            perm = jnp.where(k == (i % per) * d + i // per, 1.0, 0.0).astype(bf16)
            for sb in range(tm // PERM_ROWS):
                y = jnp.dot(perm, a_ref[sb * PERM_ROWS:(sb + 1) * PERM_ROWS, :],
                            preferred_element_type=f32).astype(bf16)
                for r in range(d):
                    dst = r * (tm // d) + sb * per
                    perm_ref[g, dst:dst + per, :] = y[r * per:(r + 1) * per, :]

    def project(lhs_ref):
        is_q = (j % col_blocks_per_group) < col_blocks_per_group // 3
        w = (w_ref[...] * jnp.where(is_q, QK_SCALE, 1.0)).astype(bf16)
        rc = ROW_CHUNK
        for r in range(tm // rc):
            rows = slice(r * rc, (r + 1) * rc)
            o_ref[rows, :] = jnp.dot(lhs_ref[rows, :], w, preferred_element_type=f32).astype(o_ref.dtype)

    grp = j // col_blocks_per_group
    for g in range(N_GROUPS - 1):
        pl.when(grp == g)(functools.partial(project, perm_ref.at[g]))


def _groupproj(u, w, width, tn=2048):
    n, k = u.shape
    c = 3 * (N_GROUPS - 1) * width
    tn = math.gcd(tn, width)
    first = 3 * width // tn
    return pl.pallas_call(
        functools.partial(_groupproj_kernel, col_blocks_per_group=3 * width // tn),
        out_shape=jax.ShapeDtypeStruct((n, c), bf16),
        grid=(n // ATT_TILE, c // tn),
        in_specs=[pl.BlockSpec((ATT_TILE, k), lambda i, j: (i, 0)),
                  pl.BlockSpec((k, tn), lambda i, j: (0, j + first))],
        out_specs=pl.BlockSpec((ATT_TILE, tn), lambda i, j: (i, j)),
        scratch_shapes=[pltpu.VMEM((N_GROUPS - 1, ATT_TILE, k), bf16)],
        compiler_params=_cparams(("parallel", "arbitrary")),
        name="group_proj",
    )(u, w)


def kernel(x, ln_g, hg_w_in, hg_lb_logits, hg_norm_g, hg_w_out, att_w_in, att_w_out, rel_bias, final_g):
    batch, seq, d_model = x.shape
    n = batch * seq
    w = hg_w_out.shape[1]
    heads = w // HEAD
    assert seq % ATT_TILE == 0 and seq % HG_TILE == 0 and w % (2 * HEAD) == 0

    lower = jnp.cumsum(jax.nn.softmax(hg_lb_logits.astype(f32), axis=0), axis=0)
    h0 = x.reshape(n, d_model)

    u0 = _rmsnorm(h0, ln_g[0], bf16)
    y0 = _hgrn(u0, hg_w_in[0], lower[0], hg_norm_g[0], batch, seq)
    h1, u1 = _outproj(y0, hg_w_out[0], h0, ln_g[1], bf16, emit_h=True)

    wa = att_w_in[0]
    y1 = _attention(u1, wa, _groupproj(u1, wa, w), _bias_tables(rel_bias, heads), batch, seq, heads)
    (out,) = _outproj(y1, att_w_out[0], h1, final_g, f32, emit_h=False)
    return out.reshape(batch, seq, d_model)
```

```python
import functools
import math

import jax
import jax.numpy as jnp
from jax import lax
from jax.experimental import pallas as pl
from jax.experimental.pallas import tpu as pltpu

EPS = 1e-6
HEAD = 128
HG_CHUNK = 64
ATT_BLOCK = 128
DILATED_GROUPS = ((128, 1), (512, 4), (2048, 16))
N_GROUPS = len(DILATED_GROUPS)
N_BUCKETS = 32
MAX_DISTANCE = 2048
ATT_TILE = ATT_BLOCK * max(d for _, d in DILATED_GROUPS)
NEG = -1e30

VMEM_LIMIT = 56 * 1024 * 1024
ROW_CHUNK = 512

f32 = jnp.float32
bf16 = jnp.bfloat16


def _cparams(sem):
    return pltpu.CompilerParams(dimension_semantics=sem, vmem_limit_bytes=VMEM_LIMIT)


def _rmsnorm_kernel(x_ref, g_ref, o_ref):
    x = x_ref[...]
    ms = jnp.mean(x * x, axis=-1, keepdims=True)
    o_ref[...] = (x * lax.rsqrt(ms + EPS) * g_ref[...]).astype(o_ref.dtype)


def _rmsnorm(x, g, out_dtype, tm=1024):
    n, d = x.shape
    return pl.pallas_call(
        _rmsnorm_kernel,
        out_shape=jax.ShapeDtypeStruct((n, d), out_dtype),
        grid=(n // tm,),
        in_specs=[pl.BlockSpec((tm, d), lambda i: (i, 0)),
                  pl.BlockSpec((1, d), lambda i: (0, 0))],
        out_specs=pl.BlockSpec((tm, d), lambda i: (i, 0)),
        compiler_params=_cparams(("parallel",)),
        name="rmsnorm",
    )(x, g.reshape(1, d))


def _outproj_kernel(y_ref, w_ref, h_ref, g_ref, *refs, emit_h):
    *out_refs, wb_ref = refs

    @pl.when(pl.program_id(0) == 0)
    def _():
        wb_ref[...] = w_ref[...].astype(bf16)

    h = h_ref[...] + jnp.dot(y_ref[...], wb_ref[...], preferred_element_type=f32)
    if emit_h:
        out_refs[0][...] = h
    n_ref = out_refs[-1]
    ms = jnp.mean(h * h, axis=-1, keepdims=True)
    n_ref[...] = (h * lax.rsqrt(ms + EPS) * g_ref[...]).astype(n_ref.dtype)


def _outproj(y, w, h, g, norm_dtype, emit_h, tm=1024):
    n, k = y.shape
    d = w.shape[1]
    tm = min(tm, n)
    row = lambda i: (i, 0)
    out_shape = [jax.ShapeDtypeStruct((n, d), norm_dtype)]
    out_specs = [pl.BlockSpec((tm, d), row)]
    if emit_h:
        out_shape.insert(0, jax.ShapeDtypeStruct((n, d), f32))
        out_specs.insert(0, pl.BlockSpec((tm, d), row))
    return pl.pallas_call(
        functools.partial(_outproj_kernel, emit_h=emit_h),
        out_shape=out_shape,
        grid=(n // tm,),
        in_specs=[pl.BlockSpec((tm, k), row),
                  pl.BlockSpec((k, d), lambda i: (0, 0)),
                  pl.BlockSpec((tm, d), row),
                  pl.BlockSpec((1, d), lambda i: (0, 0))],
        out_specs=out_specs,
        scratch_shapes=[pltpu.VMEM((k, d), bf16)],
        compiler_params=_cparams(("arbitrary",)),
        name="outproj",
    )(y, w, h, g.reshape(1, d))


HG_GROUP = 4 * HG_CHUNK
HG_BLOCK = 2 * HG_GROUP
HG_TILE = 4 * HG_BLOCK


def _sigmoid(x):
    return 1.0 / (1.0 + jnp.exp(-x))


def _hgrn_kernel(u_ref, un_ref, wq_ref, wf_ref, wi_ref, wg_ref, lb_ref, ng_ref, o_ref, st_ref, ps_ref, wb_ref):
    c = HG_CHUNK
    half = c // 2
    rg = HG_GROUP
    nck = rg // c
    w_refs = (wq_ref, wf_ref, wi_ref, wg_ref)

    def project(src_ref, row0, slot):
        u = src_ref[row0:row0 + HG_BLOCK, :]
        for k in range(len(w_refs)):
            ps_ref[slot, k] = jnp.dot(u, wb_ref[k], preferred_element_type=f32)

    @pl.when(pl.program_id(2) == 0)
    def _():
        st_ref[...] = jnp.zeros_like(st_ref)
        for k, w_ref in enumerate(w_refs):
            wb_ref[k] = w_ref[...].astype(bf16)
        project(u_ref, 0, 0)

    lb = lb_ref[...]
    ng = ng_ref[...]
    row = lax.broadcasted_iota(jnp.int32, (rg, rg), 0)
    col = lax.broadcasted_iota(jnp.int32, (rg, rg), 1)
    causal = (row >= col) & ((row // c) == (col // c))
    chunk_row = lax.broadcasted_iota(jnp.int32, (rg, 2 * HEAD), 0) % c
    zero_blk = jnp.zeros((HEAD, HEAD), bf16)
    nt = (((1,), (1,)), ((), ()))
    tn = (((0,), (0,)), ((), ()))

    def decays(slot, g):
        rows = slice(g * rg, (g + 1) * rg)
        f = lb + (1.0 - lb) * _sigmoid(ps_ref[slot, 1, rows, :])
        b = jnp.log2(f)
        shift = 1
        while shift < c:
            b = b + jnp.where(chunk_row >= shift, pltpu.roll(b, shift, axis=0), 0.0)
            shift *= 2
        return slot, rows, 1.0 - f, b

    def operands(slot, rows, k, b):
        q = ps_ref[slot, 0, rows, :]
        qd, kd, q0, decay, us = [], [], [], [], []
        for ci in range(nck):
            sl = slice(ci * c, (ci + 1) * c)
            bc = b[sl, :]
            b_mid = bc[half - 1:half, :]
            b_last = bc[c - 1:c, :]
            qm = q[sl, :] * jnp.exp2(bc - b_mid)
            qd.append(qm.astype(bf16))
            kd.append((k[sl, :] * jnp.exp2(b_mid - bc)).astype(bf16))
            q0.append((qm * jnp.exp2(b_mid)).astype(bf16))
            decay.append(jnp.exp2(b_last))
            us.append(jnp.exp2(b_last - b_mid))
        return slot, rows, qd, kd, q0, decay, us, ps_ref[slot, 2, rows, :].astype(bf16)

    def intra_chunk(slot, rows, qd, kd, q0, decay, us, v):
        qd_all = jnp.concatenate(qd, axis=0)
        kd_all = jnp.concatenate(kd, axis=0)
        intra = []
        for h in range(2):
            ln = slice(h * HEAD, (h + 1) * HEAD)
            s = lax.dot_general(qd_all[:, ln], kd_all[:, ln], nt, preferred_element_type=f32)
            s = jnp.where(causal, s, 0.0).astype(bf16)
            intra.append(jnp.dot(s, v[:, ln], preferred_element_type=f32))
        upd = [[lax.dot_general(v[ci * c:(ci + 1) * c, h * HEAD:(h + 1) * HEAD],
                                kd[ci][:, h * HEAD:(h + 1) * HEAD], tn, preferred_element_type=f32)
                * us[ci][:, h * HEAD:(h + 1) * HEAD] for h in range(2)] for ci in range(nck)]
        return slot, rows, jnp.concatenate(intra, axis=1), upd, q0, decay

    def recur(out_row0, st, slot, rows, o_intra, upd, q0, decay):
        gate = ps_ref[slot, 3, rows, :]
        for ci in range(nck):
            sl = slice(ci * c, (ci + 1) * c)
            s0 = st[0].astype(bf16)
            s1 = st[1].astype(bf16)
            both = jnp.concatenate([jnp.concatenate([s0, zero_blk], axis=1),
                                    jnp.concatenate([zero_blk, s1], axis=1)], axis=0)
            o = o_intra[sl, :] + lax.dot_general(q0[ci], both, nt, preferred_element_type=f32)
            for h in range(2):
                ln = slice(h * HEAD, (h + 1) * HEAD)
                st[h] = st[h] * decay[ci][:, ln] + upd[ci][h]
                oh = o[:, ln]
                oh = oh * lax.rsqrt(jnp.mean(oh * oh, axis=-1, keepdims=True) + EPS)
                gh = gate[sl, ln]
                r0 = out_row0 + rows.start + ci * c
                o_ref[r0:r0 + c, ln] = (oh * ng[:, ln] * (gh * _sigmoid(gh))).astype(o_ref.dtype)
        return st

    def mix(slot, out_row0, st):
        stage = [decays(slot, g) for g in range(HG_BLOCK // rg)]
        stage = [operands(*x) for x in stage]
        stage = [intra_chunk(*x) for x in stage]
        for x in stage:
            st = recur(out_row0, st, *x)
        return st

    st = [st_ref[0], st_ref[1]]
    n_blocks = HG_TILE // HG_BLOCK
    for i in range(n_blocks):
        if i + 1 < n_blocks:
            project(u_ref, (i + 1) * HG_BLOCK, (i + 1) % 2)
        else:
            project(un_ref, 0, 0)
        st = mix(i % 2, i * HG_BLOCK, st)
    st_ref[0] = st[0]
    st_ref[1] = st[1]


def _hgrn(u, w_in, lb, ng, batch, seq):
    n, dm = u.shape
    w = w_in.shape[1] // 4
    pairs = w // (2 * HEAD)
    nt = seq // HG_TILE
    blk = (HG_TILE, 2 * HEAD)

    def wspec(k):
        return pl.BlockSpec((dm, 2 * HEAD), lambda b, pr, t, k=k: (0, k * pairs + pr))

    vec = pl.BlockSpec((1, 2 * HEAD), lambda b, pr, t: (0, pr))
    return pl.pallas_call(
        _hgrn_kernel,
        out_shape=jax.ShapeDtypeStruct((n, w), bf16),
        grid=(batch, pairs, nt),
        in_specs=[pl.BlockSpec((HG_TILE, dm), lambda b, pr, t: (b * nt + t, 0)),
                  pl.BlockSpec((HG_TILE, dm), lambda b, pr, t: (b * nt + jnp.minimum(t + 1, nt - 1), 0)),
                  wspec(0), wspec(1), wspec(2), wspec(3), vec, vec],
        out_specs=pl.BlockSpec(blk, lambda b, pr, t: (b * nt + t, pr)),
        scratch_shapes=[pltpu.VMEM((2, HEAD, HEAD), f32),
                        pltpu.VMEM((2, 4, HG_BLOCK, 2 * HEAD), f32),
                        pltpu.VMEM((4, dm, 2 * HEAD), bf16)],
        compiler_params=_cparams(("parallel", "parallel", "arbitrary")),
        name="hgrn2",
    )(u, u, w_in, w_in, w_in, w_in, lb.reshape(1, w), ng.reshape(1, w))


MERGE_D = 4
LOG2E = math.log2(math.e)
QK_SCALE = HEAD ** -0.5 * LOG2E


def _attn_kernel(u_ref, wq_ref, wk_ref, wv_ref, wg_ref, q1_ref, k1_ref, v1_ref, q2_ref, k2_ref, v2_ref,
                 bias_ref, o_ref, ck0, cv0, ck1, cv1, ck2, cv2, oacc, lacc, macc, nat, p0, w0):
    t = pl.program_id(2)
    par = t % 2
    q_refs = (p0.at[0], q1_ref, q2_ref)
    k_refs = (p0.at[1], k1_ref, k2_ref)
    v_refs = (p0.at[2], v1_ref, v2_ref)
    gate_ref = p0.at[3]
    cks = (ck0, ck1, ck2)
    cvs = (cv0, cv1, cv2)
    blk = ATT_BLOCK
    res_rows = ATT_TILE // MERGE_D

    @pl.when(t == 0)
    def _():
        for g, (_, d) in enumerate(DILATED_GROUPS):
            cks[g][:, 2 * blk:, :] = jnp.zeros((d, blk, HEAD), bf16)
            cvs[g][:, 2 * blk:, :] = jnp.zeros((d, blk, HEAD), bf16)
        for k, w_ref in enumerate((wq_ref, wk_ref, wv_ref, wg_ref)):
            w0[:, k * HEAD:(k + 1) * HEAD] = (w_ref[...] * (QK_SCALE if k == 0 else 1.0)).astype(bf16)

    proj_rows = ROW_CHUNK // 2

    def project(c0):
        y = jnp.dot(u_ref[c0:c0 + proj_rows, :], w0[...], preferred_element_type=f32).astype(bf16)
        for k in range(4):
            p0[k, c0:c0 + proj_rows, :] = y[:, k * HEAD:(k + 1) * HEAD]

    slot = pl.ds(pl.multiple_of(par * 2 * blk, blk), blk)

    def stage_carry(g):
        d = DILATED_GROUPS[g][1]
        per_res = ATT_TILE // d
        for r in range(d):
            for src, dst in ((k_refs[g], cks[g]), (v_refs[g], cvs[g])):
                dst[r, blk:2 * blk, :] = src[r * per_res:r * per_res + blk, :]
                dst[r, slot, :] = src[(r + 1) * per_res - blk:(r + 1) * per_res, :]

    cwin = pl.ds(pl.multiple_of((1 - par) * blk, blk), 2 * blk)
    colid = lax.broadcasted_iota(jnp.int32, (blk, 2 * blk), 1)
    pen = jnp.where(colid >= blk, jnp.where(t == 0, NEG, 0.0).astype(f32), 0.0)

    def carry_bias(g):
        b = bias_ref[g]
        swapped = jnp.concatenate([b[:, blk:], b[:, :blk]], axis=1)
        return jnp.where(par == 1, b, swapped) + pen

    carry_biases = [carry_bias(g) for g in range(N_GROUPS)]

    def attend(q, kwin, vwin, bias):
        s = lax.dot_general(q, kwin, (((1,), (1,)), ((), ())), preferred_element_type=f32)
        s = s + bias
        m = jnp.max(s, axis=-1, keepdims=True)
        p = jnp.exp2(s - m)
        l = jnp.sum(p, axis=-1, keepdims=True)
        acc = jnp.dot(p.astype(bf16), vwin, preferred_element_type=f32)
        return acc, jnp.broadcast_to(l, (blk, HEAD)), jnp.broadcast_to(m, (blk, HEAD))

    def block(g, r, j):
        per_res = ATT_TILE // DILATED_GROUPS[g][1]
        q = q_refs[g][r * per_res + j * blk:r * per_res + (j + 1) * blk, :]
        if j == 0:
            return attend(q, cks[g][r, cwin, :], cvs[g][r, cwin, :], carry_biases[g])
        win = slice(r * per_res + (j - 1) * blk, r * per_res + (j + 1) * blk)
        return attend(q, k_refs[g][win, :], v_refs[g][win, :], bias_ref[g])

    dsts = (oacc, lacc, macc)

    def group1(r, j):
        row0 = r * (ATT_TILE // d1) + j * blk
        for k, val in enumerate(block(1, r, j)):
            dsts[k][1, row0:row0 + blk, :] = val

    def group2(r16):
        rows = pl.ds((r16 % MERGE_D) * res_rows + r16 // MERGE_D, blk, stride=d2 // MERGE_D)
        for k, val in enumerate(block(2, r16, 0)):
            dsts[k][2, rows, :] = val

    def group0(j):
        for k, val in enumerate(block(0, 0, j)):
            dsts[k][0, j * blk:(j + 1) * blk, :] = val

    d1 = DILATED_GROUPS[1][1]
    d2 = DILATED_GROUPS[2][1]
    stage_carry(1)
    stage_carry(2)
    dilated = ([functools.partial(group1, r, j) for r in range(d1) for j in range(ATT_TILE // d1 // blk)]
               + [functools.partial(group2, r16) for r16 in range(d2)])
    n_proj = ATT_TILE // proj_rows
    per_proj = len(dilated) // n_proj
    for i in range(n_proj):
        project(i * proj_rows)
        for task in dilated[i * per_proj:(i + 1) * per_proj]:
            task()
    stage_carry(0)
    for j in range(ATT_TILE // blk):
        group0(j)

    def merge(c, carry):
        for r4 in range(MERGE_D):
            rows = pl.ds(pl.multiple_of(r4 * res_rows + c * blk, blk), blk)
            toks = pl.ds(c * blk * MERGE_D + r4, blk, stride=MERGE_D)
            at = (toks, rows, rows)
            ms = [macc[g, at[g], :] for g in range(N_GROUPS)]
            mx = functools.reduce(jnp.maximum, ms)
            ws = [jnp.exp2(x - mx) for x in ms]
            num = functools.reduce(lambda a, b: a + b, [w * oacc[g, at[g], :] for g, w in enumerate(ws)])
            den = functools.reduce(lambda a, b: a + b, [w * lacc[g, at[g], :] for g, w in enumerate(ws)])
            nat[toks, :] = num / den
        return carry

    lax.fori_loop(0, res_rows // blk, merge, 0)

    rc = 256
    for c0 in range(0, ATT_TILE, rc):
        rows = slice(c0, c0 + rc)
        gate = gate_ref[rows, :].astype(f32)
        o_ref[rows, :] = (nat[rows, :] * (gate * _sigmoid(gate))).astype(o_ref.dtype)


def _attention(u, w_in, p, bias, batch, seq, heads):
    n, dm = u.shape
    nt = seq // ATT_TILE
    blk = (ATT_TILE, HEAD)

    def spec(k):
        return pl.BlockSpec(blk, lambda b, h, t, k=k: (b * nt + t, k * heads + h))

    def wspec(k):
        return pl.BlockSpec((dm, HEAD), lambda b, h, t, k=k: (0, k * heads + h))

    scratch = []
    for _, d in DILATED_GROUPS:
        scratch += [pltpu.VMEM((d, 3 * ATT_BLOCK, HEAD), bf16)] * 2
    scratch += [pltpu.VMEM((N_GROUPS, ATT_TILE, HEAD), f32)] * 3
    scratch += [pltpu.VMEM((ATT_TILE, HEAD), f32)]
    scratch += [pltpu.VMEM((4, ATT_TILE, HEAD), bf16), pltpu.VMEM((dm, 4 * HEAD), bf16)]
    return pl.pallas_call(
        _attn_kernel,
        out_shape=jax.ShapeDtypeStruct((n, heads * HEAD), bf16),
        grid=(batch, heads, nt),
        in_specs=[pl.BlockSpec((ATT_TILE, dm), lambda b, h, t: (b * nt + t, 0)),
                  wspec(0), wspec(1), wspec(2), wspec(3 * N_GROUPS)] + [spec(k) for k in range(6)] + [
                  pl.BlockSpec((N_GROUPS, None, ATT_BLOCK, 2 * ATT_BLOCK), lambda b, h, t: (0, h, 0, 0))],
        out_specs=pl.BlockSpec(blk, lambda b, h, t: (b * nt + t, h)),
        scratch_shapes=scratch,
        compiler_params=_cparams(("parallel", "parallel", "arbitrary")),
        name="dilated_attention",
    )(u, w_in, w_in, w_in, w_in, *([p] * 6), bias)


def _t5_bucket(dist):
    max_exact = N_BUCKETS // 2
    df = jnp.maximum(dist, 1).astype(f32)
    large = max_exact + (jnp.log(df / max_exact) / math.log(MAX_DISTANCE / max_exact)
                         * (N_BUCKETS - max_exact)).astype(jnp.int32)
    large = jnp.minimum(large, N_BUCKETS - 1)
    return jnp.where(dist < max_exact, dist, large)


def _bias_tables(rel_bias, heads):
    a = jnp.arange(ATT_BLOCK)[:, None]
    c = jnp.arange(2 * ATT_BLOCK)[None, :]
    rel = ATT_BLOCK + a - c
    tables = []
    for g, (window, d) in enumerate(DILATED_GROUPS):
        assert window // d == ATT_BLOCK and ATT_TILE % (d * ATT_BLOCK) == 0
        band = (rel >= 0) & (rel <= window // d)
        onehot = jax.nn.one_hot(_t5_bucket(jnp.maximum(rel, 0) * d), N_BUCKETS, dtype=f32)
        tab = jnp.einsum("acb,bh->hac", onehot, rel_bias[:, g * heads:(g + 1) * heads].astype(f32),
                         precision=lax.Precision.HIGHEST)
        tables.append(jnp.where(band[None], tab * LOG2E, NEG))
    return jnp.stack(tables, axis=0)


PERM_ROWS = 256


def _groupproj_kernel(a_ref, w_ref, o_ref, perm_ref, *, col_blocks_per_group):
    j = pl.program_id(1)
    tm = a_ref.shape[0]

    @pl.when(j == 0)
    def _():
        i = lax.broadcasted_iota(jnp.int32, (PERM_ROWS, PERM_ROWS), 0)
        k = lax.broadcasted_iota(jnp.int32, (PERM_ROWS, PERM_ROWS), 1)
        for g, (_, d) in enumerate(DILATED_GROUPS[1:]):
            per = PERM_ROWS // d
            perm = jnp.where(k == (i % per) * d + i // per, 1.0, 0.0).astype(bf16)
            for sb in range(tm // PERM_ROWS):
                y = jnp.dot(perm, a_ref[sb * PERM_ROWS:(sb + 1) * PERM_ROWS, :],
                            preferred_element_type=f32).astype(bf16)
                for r in range(d):
                    dst = r * (tm // d) + sb * per
                    perm_ref[g, dst:dst + per, :] = y[r * per:(r + 1) * per, :]

    def project(lhs_ref):
        is_q = (j % col_blocks_per_group) < col_blocks_per_group // 3
        w = (w_ref[...] * jnp.where(is_q, QK_SCALE, 1.0)).astype(bf16)
        rc = ROW_CHUNK
        for r in range(tm // rc):
            rows = slice(r * rc, (r + 1) * rc)
            o_ref[rows, :] = jnp.dot(lhs_ref[rows, :], w, preferred_element_type=f32).astype(o_ref.dtype)

    grp = j // col_blocks_per_group
    for g in range(N_GROUPS - 1):
        pl.when(grp == g)(functools.partial(project, perm_ref.at[g]))


def _groupproj(u, w, width, tn=2048):
    n, k = u.shape
    c = 3 * (N_GROUPS - 1) * width
    tn = math.gcd(tn, width)
    first = 3 * width // tn
    return pl.pallas_call(
        functools.partial(_groupproj_kernel, col_blocks_per_group=3 * width // tn),
        out_shape=jax.ShapeDtypeStruct((n, c), bf16),
        grid=(n // ATT_TILE, c // tn),
        in_specs=[pl.BlockSpec((ATT_TILE, k), lambda i, j: (i, 0)),
                  pl.BlockSpec((k, tn), lambda i, j: (0, j + first))],
        out_specs=pl.BlockSpec((ATT_TILE, tn), lambda i, j: (i, j)),
        scratch_shapes=[pltpu.VMEM((N_GROUPS - 1, ATT_TILE, k), bf16)],
        compiler_params=_cparams(("parallel", "arbitrary")),
        name="group_proj",
    )(u, w)


def kernel(x, ln_g, hg_w_in, hg_lb_logits, hg_norm_g, hg_w_out, att_w_in, att_w_out, rel_bias, final_g):
    batch, seq, d_model = x.shape
    n = batch * seq
    w = hg_w_out.shape[1]
    heads = w // HEAD
    assert seq % ATT_TILE == 0 and seq % HG_TILE == 0 and w % (2 * HEAD) == 0

    lower = jnp.cumsum(jax.nn.softmax(hg_lb_logits.astype(f32), axis=0), axis=0)
    h0 = x.reshape(n, d_model)

    u0 = _rmsnorm(h0, ln_g[0], bf16)
    y0 = _hgrn(u0, hg_w_in[0], lower[0], hg_norm_g[0], batch, seq)
    h1, u1 = _outproj(y0, hg_w_out[0], h0, ln_g[1], bf16, emit_h=True)

    wa = att_w_in[0]
    y1 = _attention(u1, wa, _groupproj(u1, wa, w), _bias_tables(rel_bias, heads), batch, seq, heads)
    (out,) = _outproj(y1, att_w_out[0], h1, final_g, f32, emit_h=False)
    return out.reshape(batch, seq, d_model)
```

```python
import functools
import math

import jax
import jax.numpy as jnp
from jax import lax
from jax.experimental import pallas as pl
from jax.experimental.pallas import tpu as pltpu

EPS = 1e-6
HEAD = 128
HG_CHUNK = 64
ATT_BLOCK = 128
DILATED_GROUPS = ((128, 1), (512, 4), (2048, 16))
N_GROUPS = len(DILATED_GROUPS)
N_BUCKETS = 32
MAX_DISTANCE = 2048
ATT_TILE = ATT_BLOCK * max(d for _, d in DILATED_GROUPS)
NEG = -1e30

VMEM_LIMIT = 56 * 1024 * 1024
ROW_CHUNK = 512

f32 = jnp.float32
bf16 = jnp.bfloat16


def _cparams(sem):
    return pltpu.CompilerParams(dimension_semantics=sem, vmem_limit_bytes=VMEM_LIMIT)


def _rmsnorm_kernel(x_ref, g_ref, o_ref):
    x = x_ref[...]
    ms = jnp.mean(x * x, axis=-1, keepdims=True)
    o_ref[...] = (x * lax.rsqrt(ms + EPS) * g_ref[...]).astype(o_ref.dtype)


def _rmsnorm(x, g, out_dtype, tm=1024):
    n, d = x.shape
    return pl.pallas_call(
        _rmsnorm_kernel,
        out_shape=jax.ShapeDtypeStruct((n, d), out_dtype),
        grid=(n // tm,),
        in_specs=[pl.BlockSpec((tm, d), lambda i: (i, 0)),
                  pl.BlockSpec((1, d), lambda i: (0, 0))],
        out_specs=pl.BlockSpec((tm, d), lambda i: (i, 0)),
        compiler_params=_cparams(("parallel",)),
        name="rmsnorm",
    )(x, g.reshape(1, d))


def _outproj_kernel(y_ref, w_ref, h_ref, g_ref, *refs, emit_h):
    *out_refs, wb_ref = refs

    @pl.when(pl.program_id(0) == 0)
    def _():
        wb_ref[...] = w_ref[...].astype(bf16)

    h = h_ref[...] + jnp.dot(y_ref[...], wb_ref[...], preferred_element_type=f32)
    if emit_h:
        out_refs[0][...] = h
    n_ref = out_refs[-1]
    ms = jnp.mean(h * h, axis=-1, keepdims=True)
    n_ref[...] = (h * lax.rsqrt(ms + EPS) * g_ref[...]).astype(n_ref.dtype)


def _outproj(y, w, h, g, norm_dtype, emit_h, tm=1024):
    n, k = y.shape
    d = w.shape[1]
    tm = min(tm, n)
    row = lambda i: (i, 0)
    out_shape = [jax.ShapeDtypeStruct((n, d), norm_dtype)]
    out_specs = [pl.BlockSpec((tm, d), row)]
    if emit_h:
        out_shape.insert(0, jax.ShapeDtypeStruct((n, d), f32))
        out_specs.insert(0, pl.BlockSpec((tm, d), row))
    return pl.pallas_call(
        functools.partial(_outproj_kernel, emit_h=emit_h),
        out_shape=out_shape,
        grid=(n // tm,),
        in_specs=[pl.BlockSpec((tm, k), row),
                  pl.BlockSpec((k, d), lambda i: (0, 0)),
                  pl.BlockSpec((tm, d), row),
                  pl.BlockSpec((1, d), lambda i: (0, 0))],
        out_specs=out_specs,
        scratch_shapes=[pltpu.VMEM((k, d), bf16)],
        compiler_params=_cparams(("arbitrary",)),
        name="outproj",
    )(y, w, h, g.reshape(1, d))


HG_GROUP = 4 * HG_CHUNK
HG_BLOCK = 2 * HG_GROUP
HG_TILE = 4 * HG_BLOCK


def _sigmoid(x):
    return 1.0 / (1.0 + jnp.exp(-x))


def _hgrn_kernel(u_ref, un_ref, wq_ref, wf_ref, wi_ref, wg_ref, lb_ref, ng_ref, o_ref, st_ref, ps_ref, wb_ref):
    c = HG_CHUNK
    half = c // 2
    rg = HG_GROUP
    nck = rg // c
    w_refs = (wq_ref, wf_ref, wi_ref, wg_ref)

    def project(src_ref, row0, slot):
        u = src_ref[row0:row0 + HG_BLOCK, :]
        for k in range(len(w_refs)):
            ps_ref[slot, k] = jnp.dot(u, wb_ref[k], preferred_element_type=f32)

    @pl.when(pl.program_id(2) == 0)
    def _():
        st_ref[...] = jnp.zeros_like(st_ref)
        for k, w_ref in enumerate(w_refs):
            wb_ref[k] = w_ref[...].astype(bf16)
        project(u_ref, 0, 0)

    lb = lb_ref[...]
    ng = ng_ref[...]
    row = lax.broadcasted_iota(jnp.int32, (rg, rg), 0)
    col = lax.broadcasted_iota(jnp.int32, (rg, rg), 1)
    causal = (row >= col) & ((row // c) == (col // c))
    chunk_row = lax.broadcasted_iota(jnp.int32, (rg, 2 * HEAD), 0) % c
    head_lane = lax.broadcasted_iota(jnp.int32, (rg, 2 * HEAD), 1) // HEAD
    zero_blk = jnp.zeros((HEAD, HEAD), bf16)
    nt = (((1,), (1,)), ((), ()))
    tn = (((0,), (0,)), ((), ()))

    def decays(slot, g):
        rows = slice(g * rg, (g + 1) * rg)
        f = lb + (1.0 - lb) * _sigmoid(ps_ref[slot, 1, rows, :])
        b = jnp.log2(f)
        shift = 1
        while shift < c:
            b = b + jnp.where(chunk_row >= shift, pltpu.roll(b, shift, axis=0), 0.0)
            shift *= 2
        return slot, rows, 1.0 - f, b

    def operands(slot, rows, k, b):
        q = ps_ref[slot, 0, rows, :]
        qd, kd, q0, decay, us = [], [], [], [], []
        for ci in range(nck):
            sl = slice(ci * c, (ci + 1) * c)
            bc = b[sl, :]
            b_mid = bc[half - 1:half, :]
            b_last = bc[c - 1:c, :]
            qm = q[sl, :] * jnp.exp2(bc - b_mid)
            qd.append(qm.astype(bf16))
            kd.append((k[sl, :] * jnp.exp2(b_mid - bc)).astype(bf16))
            q0.append((qm * jnp.exp2(b_mid)).astype(bf16))
            decay.append(jnp.exp2(b_last))
            us.append(jnp.exp2(b_last - b_mid))
        return slot, rows, qd, kd, q0, decay, us, ps_ref[slot, 2, rows, :].astype(bf16)

    def intra_chunk(slot, rows, qd, kd, q0, decay, us, v):
        qd_all = jnp.concatenate(qd, axis=0)
        kd_all = jnp.concatenate(kd, axis=0)
        scores = []
        for h in range(2):
            ln = slice(h * HEAD, (h + 1) * HEAD)
            s = lax.dot_general(qd_all[:, ln], kd_all[:, ln], nt, preferred_element_type=f32)
            scores.append(jnp.where(causal, s, 0.0).astype(bf16))
        v_diag = jnp.concatenate([jnp.where(head_lane == 0, v, jnp.zeros_like(v)),
                                  jnp.where(head_lane == 1, v, jnp.zeros_like(v))], axis=0)
        o_intra = jnp.dot(jnp.concatenate(scores, axis=1), v_diag, preferred_element_type=f32)
        upd = [[lax.dot_general(v[ci * c:(ci + 1) * c, h * HEAD:(h + 1) * HEAD],
                                kd[ci][:, h * HEAD:(h + 1) * HEAD], tn, preferred_element_type=f32)
                * us[ci][:, h * HEAD:(h + 1) * HEAD] for h in range(2)] for ci in range(nck)]
        return slot, rows, o_intra, upd, q0, decay

    def recur(out_row0, st, slot, rows, o_intra, upd, q0, decay):
        gate = ps_ref[slot, 3, rows, :]
        for ci in range(nck):
            sl = slice(ci * c, (ci + 1) * c)
            s0 = st[0].astype(bf16)
            s1 = st[1].astype(bf16)
            both = jnp.concatenate([jnp.concatenate([s0, zero_blk], axis=1),
                                    jnp.concatenate([zero_blk, s1], axis=1)], axis=0)
            o = o_intra[sl, :] + lax.dot_general(q0[ci], both, nt, preferred_element_type=f32)
            for h in range(2):
                ln = slice(h * HEAD, (h + 1) * HEAD)
                st[h] = st[h] * decay[ci][:, ln] + upd[ci][h]
                oh = o[:, ln]
                oh = oh * lax.rsqrt(jnp.mean(oh * oh, axis=-1, keepdims=True) + EPS)
                gh = gate[sl, ln]
                r0 = out_row0 + rows.start + ci * c
                o_ref[r0:r0 + c, ln] = (oh * ng[:, ln] * (gh * _sigmoid(gh))).astype(o_ref.dtype)
        return st

    def mix(slot, out_row0, st):
        stage = [decays(slot, g) for g in range(HG_BLOCK // rg)]
        stage = [operands(*x) for x in stage]
        stage = [intra_chunk(*x) for x in stage]
        for x in stage:
            st = recur(out_row0, st, *x)
        return st

    st = [st_ref[0], st_ref[1]]
    n_blocks = HG_TILE // HG_BLOCK
    for i in range(n_blocks):
        if i + 1 < n_blocks:
            project(u_ref, (i + 1) * HG_BLOCK, (i + 1) % 2)
        else:
            project(un_ref, 0, 0)
        st = mix(i % 2, i * HG_BLOCK, st)
    st_ref[0] = st[0]
    st_ref[1] = st[1]


def _hgrn(u, w_in, lb, ng, batch, seq):
    n, dm = u.shape
    w = w_in.shape[1] // 4
    pairs = w // (2 * HEAD)
    nt = seq // HG_TILE
    blk = (HG_TILE, 2 * HEAD)

    def wspec(k):
        return pl.BlockSpec((dm, 2 * HEAD), lambda b, pr, t, k=k: (0, k * pairs + pr))

    vec = pl.BlockSpec((1, 2 * HEAD), lambda b, pr, t: (0, pr))
    return pl.pallas_call(
        _hgrn_kernel,
        out_shape=jax.ShapeDtypeStruct((n, w), bf16),
        grid=(batch, pairs, nt),
        in_specs=[pl.BlockSpec((HG_TILE, dm), lambda b, pr, t: (b * nt + t, 0)),
                  pl.BlockSpec((HG_TILE, dm), lambda b, pr, t: (b * nt + jnp.minimum(t + 1, nt - 1), 0)),
                  wspec(0), wspec(1), wspec(2), wspec(3), vec, vec],
        out_specs=pl.BlockSpec(blk, lambda b, pr, t: (b * nt + t, pr)),
        scratch_shapes=[pltpu.VMEM((2, HEAD, HEAD), f32),
                        pltpu.VMEM((2, 4, HG_BLOCK, 2 * HEAD), f32),
                        pltpu.VMEM((4, dm, 2 * HEAD), bf16)],
        compiler_params=_cparams(("parallel", "parallel", "arbitrary")),
        name="hgrn2",
    )(u, u, w_in, w_in, w_in, w_in, lb.reshape(1, w), ng.reshape(1, w))


MERGE_D = 4
LOG2E = math.log2(math.e)
QK_SCALE = HEAD ** -0.5 * LOG2E


def _attn_kernel(u_ref, wq_ref, wk_ref, wv_ref, wg_ref, q1_ref, k1_ref, v1_ref, q2_ref, k2_ref, v2_ref,
                 bias_ref, o_ref, ck0, cv0, ck1, cv1, ck2, cv2, oacc, lacc, macc, nat, p0, w0):
    t = pl.program_id(2)
    par = t % 2
    q_refs = (p0.at[0], q1_ref, q2_ref)
    k_refs = (p0.at[1], k1_ref, k2_ref)
    v_refs = (p0.at[2], v1_ref, v2_ref)
    gate_ref = p0.at[3]
    cks = (ck0, ck1, ck2)
    cvs = (cv0, cv1, cv2)
    blk = ATT_BLOCK
    res_rows = ATT_TILE // MERGE_D

    @pl.when(t == 0)
    def _():
        for g, (_, d) in enumerate(DILATED_GROUPS):
            cks[g][:, 2 * blk:, :] = jnp.zeros((d, blk, HEAD), bf16)
            cvs[g][:, 2 * blk:, :] = jnp.zeros((d, blk, HEAD), bf16)
        for k, w_ref in enumerate((wq_ref, wk_ref, wv_ref, wg_ref)):
            w0[:, k * HEAD:(k + 1) * HEAD] = (w_ref[...] * (QK_SCALE if k == 0 else 1.0)).astype(bf16)

    proj_rows = ROW_CHUNK // 2

    def project(c0):
        y = jnp.dot(u_ref[c0:c0 + proj_rows, :], w0[...], preferred_element_type=f32).astype(bf16)
        for k in range(4):
            p0[k, c0:c0 + proj_rows, :] = y[:, k * HEAD:(k + 1) * HEAD]

    slot = pl.ds(pl.multiple_of(par * 2 * blk, blk), blk)

    def stage_carry(g):
        d = DILATED_GROUPS[g][1]
        per_res = ATT_TILE // d
        for r in range(d):
            for src, dst in ((k_refs[g], cks[g]), (v_refs[g], cvs[g])):
                dst[r, blk:2 * blk, :] = src[r * per_res:r * per_res + blk, :]
                dst[r, slot, :] = src[(r + 1) * per_res - blk:(r + 1) * per_res, :]

    cwin = pl.ds(pl.multiple_of((1 - par) * blk, blk), 2 * blk)
    colid = lax.broadcasted_iota(jnp.int32, (blk, 2 * blk), 1)
    pen = jnp.where(colid >= blk, jnp.where(t == 0, NEG, 0.0).astype(f32), 0.0)

    def carry_bias(g):
        b = bias_ref[g]
        swapped = jnp.concatenate([b[:, blk:], b[:, :blk]], axis=1)
        return jnp.where(par == 1, b, swapped) + pen

    carry_biases = [carry_bias(g) for g in range(N_GROUPS)]

    def attend(q, kwin, vwin, bias):
        s = lax.dot_general(q, kwin, (((1,), (1,)), ((), ())), preferred_element_type=f32)
        s = s + bias
        m = jnp.max(s, axis=-1, keepdims=True)
        p = jnp.exp2(s - m)
        l = jnp.sum(p, axis=-1, keepdims=True)
        acc = jnp.dot(p.astype(bf16), vwin, preferred_element_type=f32)
        return acc, jnp.broadcast_to(l, (blk, HEAD)), jnp.broadcast_to(m, (blk, HEAD))

    def block(g, r, j):
        per_res = ATT_TILE // DILATED_GROUPS[g][1]
        q = q_refs[g][r * per_res + j * blk:r * per_res + (j + 1) * blk, :]
        if j == 0:
            return attend(q, cks[g][r, cwin, :], cvs[g][r, cwin, :], carry_biases[g])
        win = slice(r * per_res + (j - 1) * blk, r * per_res + (j + 1) * blk)
        return attend(q, k_refs[g][win, :], v_refs[g][win, :], bias_ref[g])

    dsts = (oacc, lacc, macc)

    def group1(r, j):
        row0 = r * (ATT_TILE // d1) + j * blk
        for k, val in enumerate(block(1, r, j)):
            dsts[k][1, row0:row0 + blk, :] = val

    def group2(r16):
        rows = pl.ds((r16 % MERGE_D) * res_rows + r16 // MERGE_D, blk, stride=d2 // MERGE_D)
        for k, val in enumerate(block(2, r16, 0)):
            dsts[k][2, rows, :] = val

    def group0(j):
        for k, val in enumerate(block(0, 0, j)):
            dsts[k][0, j * blk:(j + 1) * blk, :] = val

    d1 = DILATED_GROUPS[1][1]
    d2 = DILATED_GROUPS[2][1]
    stage_carry(1)
    stage_carry(2)
    dilated = ([functools.partial(group1, r, j) for r in range(d1) for j in range(ATT_TILE // d1 // blk)]
               + [functools.partial(group2, r16) for r16 in range(d2)])
    n_proj = ATT_TILE // proj_rows
    per_proj = len(dilated) // n_proj
    for i in range(n_proj):
        project(i * proj_rows)
        for task in dilated[i * per_proj:(i + 1) * per_proj]:
            task()
    stage_carry(0)
    for j in range(ATT_TILE // blk):
        group0(j)

    def merge(c, carry):
        for r4 in range(MERGE_D):
            rows = pl.ds(pl.multiple_of(r4 * res_rows + c * blk, blk), blk)
            toks = pl.ds(c * blk * MERGE_D + r4, blk, stride=MERGE_D)
            at = (toks, rows, rows)
            ms = [macc[g, at[g], :] for g in range(N_GROUPS)]
            mx = functools.reduce(jnp.maximum, ms)
            ws = [jnp.exp2(x - mx) for x in ms]
            num = functools.reduce(lambda a, b: a + b, [w * oacc[g, at[g], :] for g, w in enumerate(ws)])
            den = functools.reduce(lambda a, b: a + b, [w * lacc[g, at[g], :] for g, w in enumerate(ws)])
            nat[toks, :] = num / den
        return carry

    lax.fori_loop(0, res_rows // blk, merge, 0)

    rc = 256
    for c0 in range(0, ATT_TILE, rc):
        rows = slice(c0, c0 + rc)
        gate = gate_ref[rows, :].astype(f32)
        o_ref[rows, :] = (nat[rows, :] * (gate * _sigmoid(gate))).astype(o_ref.dtype)


def _attention(u, w_in, p, bias, batch, seq, heads):
    n, dm = u.shape
    nt = seq // ATT_TILE
    blk = (ATT_TILE, HEAD)

    def spec(k):
        return pl.BlockSpec(blk, lambda b, h, t, k=k: (b * nt + t, k * heads + h))

    def wspec(k):
        return pl.BlockSpec((dm, HEAD), lambda b, h, t, k=k: (0, k * heads + h))

    scratch = []
    for _, d in DILATED_GROUPS:
        scratch += [pltpu.VMEM((d, 3 * ATT_BLOCK, HEAD), bf16)] * 2
    scratch += [pltpu.VMEM((N_GROUPS, ATT_TILE, HEAD), f32)] * 3
    scratch += [pltpu.VMEM((ATT_TILE, HEAD), f32)]
    scratch += [pltpu.VMEM((4, ATT_TILE, HEAD), bf16), pltpu.VMEM((dm, 4 * HEAD), bf16)]
    return pl.pallas_call(
        _attn_kernel,
        out_shape=jax.ShapeDtypeStruct((n, heads * HEAD), bf16),
        grid=(batch, heads, nt),
        in_specs=[pl.BlockSpec((ATT_TILE, dm), lambda b, h, t: (b * nt + t, 0)),
                  wspec(0), wspec(1), wspec(2), wspec(3 * N_GROUPS)] + [spec(k) for k in range(6)] + [
                  pl.BlockSpec((N_GROUPS, None, ATT_BLOCK, 2 * ATT_BLOCK), lambda b, h, t: (0, h, 0, 0))],
        out_specs=pl.BlockSpec(blk, lambda b, h, t: (b * nt + t, h)),
        scratch_shapes=scratch,
        compiler_params=_cparams(("parallel", "parallel", "arbitrary")),
        name="dilated_attention",
    )(u, w_in, w_in, w_in, w_in, *([p] * 6), bias)


def _t5_bucket(dist):
    max_exact = N_BUCKETS // 2
    df = jnp.maximum(dist, 1).astype(f32)
    large = max_exact + (jnp.log(df / max_exact) / math.log(MAX_DISTANCE / max_exact)
                         * (N_BUCKETS - max_exact)).astype(jnp.int32)
    large = jnp.minimum(large, N_BUCKETS - 1)
    return jnp.where(dist < max_exact, dist, large)


def _bias_tables(rel_bias, heads):
    a = jnp.arange(ATT_BLOCK)[:, None]
    c = jnp.arange(2 * ATT_BLOCK)[None, :]
    rel = ATT_BLOCK + a - c
    tables = []
    for g, (window, d) in enumerate(DILATED_GROUPS):
        assert window // d == ATT_BLOCK and ATT_TILE % (d * ATT_BLOCK) == 0
        band = (rel >= 0) & (rel <= window // d)
        onehot = jax.nn.one_hot(_t5_bucket(jnp.maximum(rel, 0) * d), N_BUCKETS, dtype=f32)
        tab = jnp.einsum("acb,bh->hac", onehot, rel_bias[:, g * heads:(g + 1) * heads].astype(f32),
                         precision=lax.Precision.HIGHEST)
        tables.append(jnp.where(band[None], tab * LOG2E, NEG))
    return jnp.stack(tables, axis=0)


PERM_ROWS = 256


def _groupproj_kernel(a_ref, w_ref, o_ref, perm_ref, *, col_blocks_per_group):
    j = pl.program_id(1)
    tm = a_ref.shape[0]

    @pl.when(j == 0)
    def _():
        i = lax.broadcasted_iota(jnp.int32, (PERM_ROWS, PERM_ROWS), 0)
        k = lax.broadcasted_iota(jnp.int32, (PERM_ROWS, PERM_ROWS), 1)
        for g, (_, d) in enumerate(DILATED_GROUPS[1:]):
            per = PERM_ROWS // d
            perm = jnp.where(k == (i % per) * d + i // per, 1.0, 0.0).astype(bf16)
            for sb in range(tm // PERM_ROWS):
                y = jnp.dot(perm, a_ref[sb * PERM_ROWS:(sb + 1) * PERM_ROWS, :],
                            preferred_element_type=f32).astype(bf16)
                for r in range(d):
                    dst = r * (tm // d) + sb * per
                    perm_ref[g, dst:dst + per, :] = y[r * per:(r + 1) * per, :]

    def project(lhs_ref):
        is_q = (j % col_blocks_per_group) < col_blocks_per_group // 3
        w = (w_ref[...] * jnp.where(is_q, QK_SCALE, 1.0)).astype(bf16)
        rc = ROW_CHUNK
        for r in range(tm // rc):
            rows = slice(r * rc, (r + 1) * rc)
            o_ref[rows, :] = jnp.dot(lhs_ref[rows, :], w, preferred_element_type=f32).astype(o_ref.dtype)

    grp = j // col_blocks_per_group
    for g in range(N_GROUPS - 1):
        pl.when(grp == g)(functools.partial(project, perm_ref.at[g]))


def _groupproj(u, w, width, tn=2048):
    n, k = u.shape
    c = 3 * (N_GROUPS - 1) * width
    tn = math.gcd(tn, width)
    first = 3 * width // tn
    return pl.pallas_call(
        functools.partial(_groupproj_kernel, col_blocks_per_group=3 * width // tn),
        out_shape=jax.ShapeDtypeStruct((n, c), bf16),
        grid=(n // ATT_TILE, c // tn),
        in_specs=[pl.BlockSpec((ATT_TILE, k), lambda i, j: (i, 0)),
                  pl.BlockSpec((k, tn), lambda i, j: (0, j + first))],
        out_specs=pl.BlockSpec((ATT_TILE, tn), lambda i, j: (i, j)),
        scratch_shapes=[pltpu.VMEM((N_GROUPS - 1, ATT_TILE, k), bf16)],
        compiler_params=_cparams(("parallel", "arbitrary")),
        name="group_proj",
    )(u, w)


def kernel(x, ln_g, hg_w_in, hg_lb_logits, hg_norm_g, hg_w_out, att_w_in, att_w_out, rel_bias, final_g):
    batch, seq, d_model = x.shape
    n = batch * seq
    w = hg_w_out.shape[1]
    heads = w // HEAD
    assert seq % ATT_TILE == 0 and seq % HG_TILE == 0 and w % (2 * HEAD) == 0

    lower = jnp.cumsum(jax.nn.softmax(hg_lb_logits.astype(f32), axis=0), axis=0)
    h0 = x.reshape(n, d_model)

    u0 = _rmsnorm(h0, ln_g[0], bf16)
    y0 = _hgrn(u0, hg_w_in[0], lower[0], hg_norm_g[0], batch, seq)
    h1, u1 = _outproj(y0, hg_w_out[0], h0, ln_g[1], bf16, emit_h=True)

    wa = att_w_in[0]
    y1 = _attention(u1, wa, _groupproj(u1, wa, w), _bias_tables(rel_bias, heads), batch, seq, heads)
    (out,) = _outproj(y1, att_w_out[0], h1, final_g, f32, emit_h=False)
    return out.reshape(batch, seq, d_model)
```

```python
import functools
import math

import jax
import jax.numpy as jnp
from jax import lax
from jax.experimental import pallas as pl
from jax.experimental.pallas import tpu as pltpu

EPS = 1e-6
HEAD = 128
HG_CHUNK = 64
ATT_BLOCK = 128
DILATED_GROUPS = ((128, 1), (512, 4), (2048, 16))
N_GROUPS = len(DILATED_GROUPS)
N_BUCKETS = 32
MAX_DISTANCE = 2048
ATT_TILE = ATT_BLOCK * max(d for _, d in DILATED_GROUPS)
NEG = -1e30

VMEM_LIMIT = 56 * 1024 * 1024
ROW_CHUNK = 512

f32 = jnp.float32
bf16 = jnp.bfloat16


def _cparams(sem):
    return pltpu.CompilerParams(dimension_semantics=sem, vmem_limit_bytes=VMEM_LIMIT)


def _rmsnorm_kernel(x_ref, g_ref, o_ref):
    x = x_ref[...]
    ms = jnp.mean(x * x, axis=-1, keepdims=True)
    o_ref[...] = (x * lax.rsqrt(ms + EPS) * g_ref[...]).astype(o_ref.dtype)


def _rmsnorm(x, g, out_dtype, tm=1024):
    n, d = x.shape
    return pl.pallas_call(
        _rmsnorm_kernel,
        out_shape=jax.ShapeDtypeStruct((n, d), out_dtype),
        grid=(n // tm,),
        in_specs=[pl.BlockSpec((tm, d), lambda i: (i, 0)),
                  pl.BlockSpec((1, d), lambda i: (0, 0))],
        out_specs=pl.BlockSpec((tm, d), lambda i: (i, 0)),
        compiler_params=_cparams(("parallel",)),
        name="rmsnorm",
    )(x, g.reshape(1, d))


def _outproj_kernel(y_hbm, w_ref, h_hbm, g_ref, *refs, emit_h, tm):
    *out_hbm, wb_ref = refs
    wb_ref[...] = w_ref[...].astype(bf16)
    n, k = y_hbm.shape
    d = w_ref.shape[1]

    def tile(y_ref, h_ref, *out_refs):
        h = h_ref[...] + jnp.dot(y_ref[...], wb_ref[...], preferred_element_type=f32)
        if emit_h:
            out_refs[0][...] = h
        n_ref = out_refs[-1]
        ms = jnp.mean(h * h, axis=-1, keepdims=True)
        n_ref[...] = (h * lax.rsqrt(ms + EPS) * g_ref[...]).astype(n_ref.dtype)

    row = lambda i: (i, 0)
    deep = pl.Buffered(3)
    pltpu.emit_pipeline(
        tile,
        grid=(n // tm,),
        in_specs=[pl.BlockSpec((tm, k), row, pipeline_mode=deep), pl.BlockSpec((tm, d), row, pipeline_mode=deep)],
        out_specs=[pl.BlockSpec((tm, d), row)] * len(out_hbm),
    )(y_hbm, h_hbm, *out_hbm)


def _outproj(y, w, h, g, norm_dtype, emit_h, tm=1024):
    n, k = y.shape
    d = w.shape[1]
    tm = min(tm, n)
    out_shape = [jax.ShapeDtypeStruct((n, d), norm_dtype)]
    if emit_h:
        out_shape.insert(0, jax.ShapeDtypeStruct((n, d), f32))
    hbm = pl.BlockSpec(memory_space=pl.ANY)
    vmem = pl.BlockSpec(memory_space=pltpu.VMEM)
    return pl.pallas_call(
        functools.partial(_outproj_kernel, emit_h=emit_h, tm=tm),
        out_shape=out_shape,
        in_specs=[hbm, vmem, hbm, vmem],
        out_specs=[hbm] * len(out_shape),
        scratch_shapes=[pltpu.VMEM((k, d), bf16)],
        compiler_params=pltpu.CompilerParams(vmem_limit_bytes=VMEM_LIMIT),
        name="outproj",
    )(y, w, h, g.reshape(1, d))


HG_GROUP = 4 * HG_CHUNK
HG_BLOCK = 2 * HG_GROUP
HG_TILE = 4 * HG_BLOCK


def _sigmoid(x):
    return 1.0 / (1.0 + jnp.exp(-x))


def _hgrn_kernel(u_ref, un_ref, wq_ref, wf_ref, wi_ref, wg_ref, lb_ref, ng_ref, o_ref, st_ref, ps_ref, wb_ref):
    c = HG_CHUNK
    half = c // 2
    rg = HG_GROUP
    nck = rg // c
    w_refs = (wq_ref, wf_ref, wi_ref, wg_ref)

    def project(src_ref, row0, slot):
        u = src_ref[row0:row0 + HG_BLOCK, :]
        for k in range(len(w_refs)):
            ps_ref[slot, k] = jnp.dot(u, wb_ref[k], preferred_element_type=f32)

    @pl.when(pl.program_id(2) == 0)
    def _():
        st_ref[...] = jnp.zeros_like(st_ref)
        for k, w_ref in enumerate(w_refs):
            wb_ref[k] = w_ref[...].astype(bf16)
        project(u_ref, 0, 0)

    lb = lb_ref[...]
    ng = ng_ref[...]
    row = lax.broadcasted_iota(jnp.int32, (rg, rg), 0)
    col = lax.broadcasted_iota(jnp.int32, (rg, rg), 1)
    causal = (row >= col) & ((row // c) == (col // c))
    chunk_row = lax.broadcasted_iota(jnp.int32, (rg, 2 * HEAD), 0) % c
    head_lane = lax.broadcasted_iota(jnp.int32, (rg, 2 * HEAD), 1) // HEAD
    zero_blk = jnp.zeros((HEAD, HEAD), bf16)
    nt = (((1,), (1,)), ((), ()))
    tn = (((0,), (0,)), ((), ()))

    def decays(slot, g):
        rows = slice(g * rg, (g + 1) * rg)
        f = lb + (1.0 - lb) * _sigmoid(ps_ref[slot, 1, rows, :])
        b = jnp.log2(f)
        shift = 1
        while shift < c:
            b = b + jnp.where(chunk_row >= shift, pltpu.roll(b, shift, axis=0), 0.0)
            shift *= 2
        return slot, rows, 1.0 - f, b

    def operands(slot, rows, k, b):
        q = ps_ref[slot, 0, rows, :]
        qd, kd, q0, decay, us = [], [], [], [], []
        for ci in range(nck):
            sl = slice(ci * c, (ci + 1) * c)
            bc = b[sl, :]
            b_mid = bc[half - 1:half, :]
            b_last = bc[c - 1:c, :]
            qm = q[sl, :] * jnp.exp2(bc - b_mid)
            qd.append(qm.astype(bf16))
            kd.append((k[sl, :] * jnp.exp2(b_mid - bc)).astype(bf16))
            q0.append((qm * jnp.exp2(b_mid)).astype(bf16))
            decay.append(jnp.exp2(b_last))
            us.append(jnp.exp2(b_last - b_mid))
        return slot, rows, qd, kd, q0, decay, us, ps_ref[slot, 2, rows, :].astype(bf16)

    def intra_chunk(slot, rows, qd, kd, q0, decay, us, v):
        qd_all = jnp.concatenate(qd, axis=0)
        kd_all = jnp.concatenate(kd, axis=0)
        scores = []
        for h in range(2):
            ln = slice(h * HEAD, (h + 1) * HEAD)
            s = lax.dot_general(qd_all[:, ln], kd_all[:, ln], nt, preferred_element_type=f32)
            scores.append(jnp.where(causal, s, 0.0).astype(bf16))
        v_diag = jnp.concatenate([jnp.where(head_lane == 0, v, jnp.zeros_like(v)),
                                  jnp.where(head_lane == 1, v, jnp.zeros_like(v))], axis=0)
        o_intra = jnp.dot(jnp.concatenate(scores, axis=1), v_diag, preferred_element_type=f32)
        upd = [[lax.dot_general(v[ci * c:(ci + 1) * c, h * HEAD:(h + 1) * HEAD],
                                kd[ci][:, h * HEAD:(h + 1) * HEAD], tn, preferred_element_type=f32)
                * us[ci][:, h * HEAD:(h + 1) * HEAD] for h in range(2)] for ci in range(nck)]
        return slot, rows, o_intra, upd, q0, decay

    def recur(out_row0, st, slot, rows, o_intra, upd, q0, decay):
        gate = ps_ref[slot, 3, rows, :]
        for ci in range(nck):
            sl = slice(ci * c, (ci + 1) * c)
            s0 = st[0].astype(bf16)
            s1 = st[1].astype(bf16)
            both = jnp.concatenate([jnp.concatenate([s0, zero_blk], axis=1),
                                    jnp.concatenate([zero_blk, s1], axis=1)], axis=0)
            o = o_intra[sl, :] + lax.dot_general(q0[ci], both, nt, preferred_element_type=f32)
            for h in range(2):
                ln = slice(h * HEAD, (h + 1) * HEAD)
                st[h] = st[h] * decay[ci][:, ln] + upd[ci][h]
                oh = o[:, ln]
                oh = oh * lax.rsqrt(jnp.mean(oh * oh, axis=-1, keepdims=True) + EPS)
                gh = gate[sl, ln]
                r0 = out_row0 + rows.start + ci * c
                o_ref[r0:r0 + c, ln] = (oh * ng[:, ln] * (gh * _sigmoid(gh))).astype(o_ref.dtype)
        return st

    def mix(slot, out_row0, st):
        stage = [decays(slot, g) for g in range(HG_BLOCK // rg)]
        stage = [operands(*x) for x in stage]
        stage = [intra_chunk(*x) for x in stage]
        for x in stage:
            st = recur(out_row0, st, *x)
        return st

    st = [st_ref[0], st_ref[1]]
    n_blocks = HG_TILE // HG_BLOCK
    for i in range(n_blocks):
        if i + 1 < n_blocks:
            project(u_ref, (i + 1) * HG_BLOCK, (i + 1) % 2)
        else:
            project(un_ref, 0, 0)
        st = mix(i % 2, i * HG_BLOCK, st)
    st_ref[0] = st[0]
    st_ref[1] = st[1]


def _hgrn(u, w_in, lb, ng, batch, seq):
    n, dm = u.shape
    w = w_in.shape[1] // 4
    pairs = w // (2 * HEAD)
    nt = seq // HG_TILE
    blk = (HG_TILE, 2 * HEAD)

    def wspec(k):
        return pl.BlockSpec((dm, 2 * HEAD), lambda b, pr, t, k=k: (0, k * pairs + pr))

    vec = pl.BlockSpec((1, 2 * HEAD), lambda b, pr, t: (0, pr))
    return pl.pallas_call(
        _hgrn_kernel,
        out_shape=jax.ShapeDtypeStruct((n, w), bf16),
        grid=(batch, pairs, nt),
        in_specs=[pl.BlockSpec((HG_TILE, dm), lambda b, pr, t: (b * nt + t, 0)),
                  pl.BlockSpec((HG_TILE, dm), lambda b, pr, t: (b * nt + jnp.minimum(t + 1, nt - 1), 0)),
                  wspec(0), wspec(1), wspec(2), wspec(3), vec, vec],
        out_specs=pl.BlockSpec(blk, lambda b, pr, t: (b * nt + t, pr)),
        scratch_shapes=[pltpu.VMEM((2, HEAD, HEAD), f32),
                        pltpu.VMEM((2, 4, HG_BLOCK, 2 * HEAD), f32),
                        pltpu.VMEM((4, dm, 2 * HEAD), bf16)],
        compiler_params=_cparams(("parallel", "parallel", "arbitrary")),
        name="hgrn2",
    )(u, u, w_in, w_in, w_in, w_in, lb.reshape(1, w), ng.reshape(1, w))


MERGE_D = 4
LOG2E = math.log2(math.e)
QK_SCALE = HEAD ** -0.5 * LOG2E


def _attn_kernel(u_ref, wq_ref, wk_ref, wv_ref, wg_ref, q1_ref, k1_ref, v1_ref, q2_ref, k2_ref, v2_ref,
                 bias_ref, o_ref, ck0, cv0, ck1, cv1, ck2, cv2, oacc, lacc, macc, nat, p0, w0):
    t = pl.program_id(2)
    par = t % 2
    q_refs = (p0.at[0], q1_ref, q2_ref)
    k_refs = (p0.at[1], k1_ref, k2_ref)
    v_refs = (p0.at[2], v1_ref, v2_ref)
    gate_ref = p0.at[3]
    cks = (ck0, ck1, ck2)
    cvs = (cv0, cv1, cv2)
    blk = ATT_BLOCK
    res_rows = ATT_TILE // MERGE_D

    @pl.when(t == 0)
    def _():
        for g, (_, d) in enumerate(DILATED_GROUPS):
            cks[g][:, 2 * blk:, :] = jnp.zeros((d, blk, HEAD), bf16)
            cvs[g][:, 2 * blk:, :] = jnp.zeros((d, blk, HEAD), bf16)
        for k, w_ref in enumerate((wq_ref, wk_ref, wv_ref, wg_ref)):
            w0[:, k * HEAD:(k + 1) * HEAD] = (w_ref[...] * (QK_SCALE if k == 0 else 1.0)).astype(bf16)

    proj_rows = ROW_CHUNK // 2

    def project(c0):
        y = jnp.dot(u_ref[c0:c0 + proj_rows, :], w0[...], preferred_element_type=f32).astype(bf16)
        for k in range(4):
            p0[k, c0:c0 + proj_rows, :] = y[:, k * HEAD:(k + 1) * HEAD]

    slot = pl.ds(pl.multiple_of(par * 2 * blk, blk), blk)

    def stage_carry(g):
        d = DILATED_GROUPS[g][1]
        per_res = ATT_TILE // d
        for r in range(d):
            for src, dst in ((k_refs[g], cks[g]), (v_refs[g], cvs[g])):
                dst[r, blk:2 * blk, :] = src[r * per_res:r * per_res + blk, :]
                dst[r, slot, :] = src[(r + 1) * per_res - blk:(r + 1) * per_res, :]

    cwin = pl.ds(pl.multiple_of((1 - par) * blk, blk), 2 * blk)
    colid = lax.broadcasted_iota(jnp.int32, (blk, 2 * blk), 1)
    pen = jnp.where(colid >= blk, jnp.where(t == 0, NEG, 0.0).astype(f32), 0.0)

    def carry_bias(g):
        b = bias_ref[g]
        swapped = jnp.concatenate([b[:, blk:], b[:, :blk]], axis=1)
        return jnp.where(par == 1, b, swapped) + pen

    carry_biases = [carry_bias(g) for g in range(N_GROUPS)]

    def attend(q, kwin, vwin, bias):
        s = lax.dot_general(q, kwin, (((1,), (1,)), ((), ())), preferred_element_type=f32)
        s = s + bias
        m = jnp.max(s, axis=-1, keepdims=True)
        p = jnp.exp2(s - m)
        l = jnp.sum(p, axis=-1, keepdims=True)
        acc = jnp.dot(p.astype(bf16), vwin, preferred_element_type=f32)
        return acc, jnp.broadcast_to(l, (blk, HEAD)), jnp.broadcast_to(m, (blk, HEAD))

    def block(g, r, j):
        per_res = ATT_TILE // DILATED_GROUPS[g][1]
        q = q_refs[g][r * per_res + j * blk:r * per_res + (j + 1) * blk, :]
        if j == 0:
            return attend(q, cks[g][r, cwin, :], cvs[g][r, cwin, :], carry_biases[g])
        win = slice(r * per_res + (j - 1) * blk, r * per_res + (j + 1) * blk)
        return attend(q, k_refs[g][win, :], v_refs[g][win, :], bias_ref[g])

    dsts = (oacc, lacc, macc)

    def group1(r, j):
        row0 = r * (ATT_TILE // d1) + j * blk
        for k, val in enumerate(block(1, r, j)):
            dsts[k][1, row0:row0 + blk, :] = val

    def group2(r16):
        rows = pl.ds((r16 % MERGE_D) * res_rows + r16 // MERGE_D, blk, stride=d2 // MERGE_D)
        for k, val in enumerate(block(2, r16, 0)):
            dsts[k][2, rows, :] = val

    def group0(j):
        for k, val in enumerate(block(0, 0, j)):
            dsts[k][0, j * blk:(j + 1) * blk, :] = val

    d1 = DILATED_GROUPS[1][1]
    d2 = DILATED_GROUPS[2][1]
    stage_carry(1)
    stage_carry(2)
    dilated = ([functools.partial(group1, r, j) for r in range(d1) for j in range(ATT_TILE // d1 // blk)]
               + [functools.partial(group2, r16) for r16 in range(d2)])
    n_proj = ATT_TILE // proj_rows
    per_proj = len(dilated) // n_proj
    for i in range(n_proj):
        project(i * proj_rows)
        for task in dilated[i * per_proj:(i + 1) * per_proj]:
            task()
    stage_carry(0)
    for j in range(ATT_TILE // blk):
        group0(j)

    def merge(c, carry):
        for r4 in range(MERGE_D):
            rows = pl.ds(pl.multiple_of(r4 * res_rows + c * blk, blk), blk)
            toks = pl.ds(c * blk * MERGE_D + r4, blk, stride=MERGE_D)
            at = (toks, rows, rows)
            ms = [macc[g, at[g], :] for g in range(N_GROUPS)]
            mx = functools.reduce(jnp.maximum, ms)
            ws = [jnp.exp2(x - mx) for x in ms]
            num = functools.reduce(lambda a, b: a + b, [w * oacc[g, at[g], :] for g, w in enumerate(ws)])
            den = functools.reduce(lambda a, b: a + b, [w * lacc[g, at[g], :] for g, w in enumerate(ws)])
            nat[toks, :] = num / den
        return carry

    lax.fori_loop(0, res_rows // blk, merge, 0)

    rc = 256
    for c0 in range(0, ATT_TILE, rc):
        rows = slice(c0, c0 + rc)
        gate = gate_ref[rows, :].astype(f32)
        o_ref[rows, :] = (nat[rows, :] * (gate * _sigmoid(gate))).astype(o_ref.dtype)


def _attention(u, w_in, p, bias, batch, seq, heads):
    n, dm = u.shape
    nt = seq // ATT_TILE
    blk = (ATT_TILE, HEAD)

    def spec(k):
        return pl.BlockSpec(blk, lambda b, h, t, k=k: (b * nt + t, k * heads + h))

    def wspec(k):
        return pl.BlockSpec((dm, HEAD), lambda b, h, t, k=k: (0, k * heads + h))

    scratch = []
    for _, d in DILATED_GROUPS:
        scratch += [pltpu.VMEM((d, 3 * ATT_BLOCK, HEAD), bf16)] * 2
    scratch += [pltpu.VMEM((N_GROUPS, ATT_TILE, HEAD), f32)] * 3
    scratch += [pltpu.VMEM((ATT_TILE, HEAD), f32)]
    scratch += [pltpu.VMEM((4, ATT_TILE, HEAD), bf16), pltpu.VMEM((dm, 4 * HEAD), bf16)]
    return pl.pallas_call(
        _attn_kernel,
        out_shape=jax.ShapeDtypeStruct((n, heads * HEAD), bf16),
        grid=(batch, heads, nt),
        in_specs=[pl.BlockSpec((ATT_TILE, dm), lambda b, h, t: (b * nt + t, 0)),
                  wspec(0), wspec(1), wspec(2), wspec(3 * N_GROUPS)] + [spec(k) for k in range(6)] + [
                  pl.BlockSpec((N_GROUPS, None, ATT_BLOCK, 2 * ATT_BLOCK), lambda b, h, t: (0, h, 0, 0))],
        out_specs=pl.BlockSpec(blk, lambda b, h, t: (b * nt + t, h)),
        scratch_shapes=scratch,
        compiler_params=_cparams(("parallel", "parallel", "arbitrary")),
        name="dilated_attention",
    )(u, w_in, w_in, w_in, w_in, *([p] * 6), bias)


def _t5_bucket(dist):
    max_exact = N_BUCKETS // 2
    df = jnp.maximum(dist, 1).astype(f32)
    large = max_exact + (jnp.log(df / max_exact) / math.log(MAX_DISTANCE / max_exact)
                         * (N_BUCKETS - max_exact)).astype(jnp.int32)
    large = jnp.minimum(large, N_BUCKETS - 1)
    return jnp.where(dist < max_exact, dist, large)


def _bias_tables(rel_bias, heads):
    a = jnp.arange(ATT_BLOCK)[:, None]
    c = jnp.arange(2 * ATT_BLOCK)[None, :]
    rel = ATT_BLOCK + a - c
    tables = []
    for g, (window, d) in enumerate(DILATED_GROUPS):
        assert window // d == ATT_BLOCK and ATT_TILE % (d * ATT_BLOCK) == 0
        band = (rel >= 0) & (rel <= window // d)
        onehot = jax.nn.one_hot(_t5_bucket(jnp.maximum(rel, 0) * d), N_BUCKETS, dtype=f32)
        tab = jnp.einsum("acb,bh->hac", onehot, rel_bias[:, g * heads:(g + 1) * heads].astype(f32),
                         precision=lax.Precision.HIGHEST)
        tables.append(jnp.where(band[None], tab * LOG2E, NEG))
    return jnp.stack(tables, axis=0)


PERM_ROWS = 256


def _groupproj_kernel(a_ref, w_ref, o_ref, perm_ref, *, col_blocks_per_group):
    j = pl.program_id(1)
    tm = a_ref.shape[0]

    @pl.when(j == 0)
    def _():
        i = lax.broadcasted_iota(jnp.int32, (PERM_ROWS, PERM_ROWS), 0)
        k = lax.broadcasted_iota(jnp.int32, (PERM_ROWS, PERM_ROWS), 1)
        for g, (_, d) in enumerate(DILATED_GROUPS[1:]):
            per = PERM_ROWS // d
            perm = jnp.where(k == (i % per) * d + i // per, 1.0, 0.0).astype(bf16)
            for sb in range(tm // PERM_ROWS):
                y = jnp.dot(perm, a_ref[sb * PERM_ROWS:(sb + 1) * PERM_ROWS, :],
                            preferred_element_type=f32).astype(bf16)
                for r in range(d):
                    dst = r * (tm // d) + sb * per
                    perm_ref[g, dst:dst + per, :] = y[r * per:(r + 1) * per, :]

    def project(lhs_ref):
        is_q = (j % col_blocks_per_group) < col_blocks_per_group // 3
        w = (w_ref[...] * jnp.where(is_q, QK_SCALE, 1.0)).astype(bf16)
        rc = ROW_CHUNK
        for r in range(tm // rc):
            rows = slice(r * rc, (r + 1) * rc)
            o_ref[rows, :] = jnp.dot(lhs_ref[rows, :], w, preferred_element_type=f32).astype(o_ref.dtype)

    grp = j // col_blocks_per_group
    for g in range(N_GROUPS - 1):
        pl.when(grp == g)(functools.partial(project, perm_ref.at[g]))


def _groupproj(u, w, width, tn=2048):
    n, k = u.shape
    c = 3 * (N_GROUPS - 1) * width
    tn = math.gcd(tn, width)
    first = 3 * width // tn
    return pl.pallas_call(
        functools.partial(_groupproj_kernel, col_blocks_per_group=3 * width // tn),
        out_shape=jax.ShapeDtypeStruct((n, c), bf16),
        grid=(n // ATT_TILE, c // tn),
        in_specs=[pl.BlockSpec((ATT_TILE, k), lambda i, j: (i, 0)),
                  pl.BlockSpec((k, tn), lambda i, j: (0, j + first))],
        out_specs=pl.BlockSpec((ATT_TILE, tn), lambda i, j: (i, j)),
        scratch_shapes=[pltpu.VMEM((N_GROUPS - 1, ATT_TILE, k), bf16)],
        compiler_params=_cparams(("parallel", "arbitrary")),
        name="group_proj",
    )(u, w)


def kernel(x, ln_g, hg_w_in, hg_lb_logits, hg_norm_g, hg_w_out, att_w_in, att_w_out, rel_bias, final_g):
    batch, seq, d_model = x.shape
    n = batch * seq
    w = hg_w_out.shape[1]
    heads = w // HEAD
    assert seq % ATT_TILE == 0 and seq % HG_TILE == 0 and w % (2 * HEAD) == 0

    lower = jnp.cumsum(jax.nn.softmax(hg_lb_logits.astype(f32), axis=0), axis=0)
    h0 = x.reshape(n, d_model)

    u0 = _rmsnorm(h0, ln_g[0], bf16)
    y0 = _hgrn(u0, hg_w_in[0], lower[0], hg_norm_g[0], batch, seq)
    h1, u1 = _outproj(y0, hg_w_out[0], h0, ln_g[1], bf16, emit_h=True)

    wa = att_w_in[0]
    y1 = _attention(u1, wa, _groupproj(u1, wa, w), _bias_tables(rel_bias, heads), batch, seq, heads)
    (out,) = _outproj(y1, att_w_out[0], h1, final_g, f32, emit_h=False)
    return out.reshape(batch, seq, d_model)
```

```python
import functools
import math

import jax
import jax.numpy as jnp
from jax import lax
from jax.experimental import pallas as pl
from jax.experimental.pallas import tpu as pltpu

EPS = 1e-6
HEAD = 128
HG_CHUNK = 64
ATT_BLOCK = 128
DILATED_GROUPS = ((128, 1), (512, 4), (2048, 16))
N_GROUPS = len(DILATED_GROUPS)
N_BUCKETS = 32
MAX_DISTANCE = 2048
ATT_TILE = ATT_BLOCK * max(d for _, d in DILATED_GROUPS)
NEG = -1e30

VMEM_LIMIT = 56 * 1024 * 1024
ROW_CHUNK = 512

f32 = jnp.float32
bf16 = jnp.bfloat16


def _cparams(sem):
    return pltpu.CompilerParams(dimension_semantics=sem, vmem_limit_bytes=VMEM_LIMIT)


def _rmsnorm_kernel(x_ref, g_ref, o_ref):
    x = x_ref[...]
    ms = jnp.mean(x * x, axis=-1, keepdims=True)
    o_ref[...] = (x * lax.rsqrt(ms + EPS) * g_ref[...]).astype(o_ref.dtype)


def _rmsnorm(x, g, out_dtype, tm=1024):
    n, d = x.shape
    return pl.pallas_call(
        _rmsnorm_kernel,
        out_shape=jax.ShapeDtypeStruct((n, d), out_dtype),
        grid=(n // tm,),
        in_specs=[pl.BlockSpec((tm, d), lambda i: (i, 0)),
                  pl.BlockSpec((1, d), lambda i: (0, 0))],
        out_specs=pl.BlockSpec((tm, d), lambda i: (i, 0)),
        compiler_params=_cparams(("parallel",)),
        name="rmsnorm",
    )(x, g.reshape(1, d))


def _outproj_kernel(y_hbm, w_ref, h_hbm, g_ref, *refs, emit_h, tm):
    *out_hbm, wb_ref = refs
    wb_ref[...] = w_ref[...].astype(bf16)
    n, k = y_hbm.shape
    d = w_ref.shape[1]

    def tile(y_ref, h_ref, *out_refs):
        h = h_ref[...] + jnp.dot(y_ref[...], wb_ref[...], preferred_element_type=f32)
        if emit_h:
            out_refs[0][...] = h
        n_ref = out_refs[-1]
        ms = jnp.mean(h * h, axis=-1, keepdims=True)
        n_ref[...] = (h * lax.rsqrt(ms + EPS) * g_ref[...]).astype(n_ref.dtype)

    row = lambda i: (i, 0)
    deep = pl.Buffered(3)
    pltpu.emit_pipeline(
        tile,
        grid=(n // tm,),
        in_specs=[pl.BlockSpec((tm, k), row, pipeline_mode=deep), pl.BlockSpec((tm, d), row, pipeline_mode=deep)],
        out_specs=[pl.BlockSpec((tm, d), row)] * len(out_hbm),
    )(y_hbm, h_hbm, *out_hbm)


def _outproj(y, w, h, g, norm_dtype, emit_h, tm=1024):
    n, k = y.shape
    d = w.shape[1]
    tm = min(tm, n)
    out_shape = [jax.ShapeDtypeStruct((n, d), norm_dtype)]
    if emit_h:
        out_shape.insert(0, jax.ShapeDtypeStruct((n, d), f32))
    hbm = pl.BlockSpec(memory_space=pl.ANY)
    vmem = pl.BlockSpec(memory_space=pltpu.VMEM)
    return pl.pallas_call(
        functools.partial(_outproj_kernel, emit_h=emit_h, tm=tm),
        out_shape=out_shape,
        in_specs=[hbm, vmem, hbm, vmem],
        out_specs=[hbm] * len(out_shape),
        scratch_shapes=[pltpu.VMEM((k, d), bf16)],
        compiler_params=pltpu.CompilerParams(vmem_limit_bytes=VMEM_LIMIT),
        name="outproj",
    )(y, w, h, g.reshape(1, d))


HG_GROUP = 4 * HG_CHUNK
HG_BLOCK = 2 * HG_GROUP
HG_TILE = 8 * HG_BLOCK


def _sigmoid(x):
    return 1.0 / (1.0 + jnp.exp(-x))


def _hgrn_kernel(u_ref, un_ref, wq_ref, wf_ref, wi_ref, wg_ref, lb_ref, ng_ref, o_ref, st_ref, ps_ref, wb_ref):
    c = HG_CHUNK
    half = c // 2
    rg = HG_GROUP
    nck = rg // c
    w_refs = (wq_ref, wf_ref, wi_ref, wg_ref)

    def project(src_ref, row0, slot):
        u = src_ref[row0:row0 + HG_BLOCK, :]
        for k in range(len(w_refs)):
            ps_ref[slot, k] = jnp.dot(u, wb_ref[k], preferred_element_type=f32)

    @pl.when(pl.program_id(2) == 0)
    def _():
        st_ref[...] = jnp.zeros_like(st_ref)
        for k, w_ref in enumerate(w_refs):
            wb_ref[k] = w_ref[...].astype(bf16)
        project(u_ref, 0, 0)

    lb = lb_ref[...]
    ng = ng_ref[...]
    row = lax.broadcasted_iota(jnp.int32, (rg, rg), 0)
    col = lax.broadcasted_iota(jnp.int32, (rg, rg), 1)
    causal = (row >= col) & ((row // c) == (col // c))
    chunk_row = lax.broadcasted_iota(jnp.int32, (rg, 2 * HEAD), 0) % c
    head_lane = lax.broadcasted_iota(jnp.int32, (rg, 2 * HEAD), 1) // HEAD
    zero_blk = jnp.zeros((HEAD, HEAD), bf16)
    nt = (((1,), (1,)), ((), ()))
    tn = (((0,), (0,)), ((), ()))

    def decays(slot, g):
        rows = slice(g * rg, (g + 1) * rg)
        f = lb + (1.0 - lb) * _sigmoid(ps_ref[slot, 1, rows, :])
        b = jnp.log2(f)
        shift = 1
        while shift < c:
            b = b + jnp.where(chunk_row >= shift, pltpu.roll(b, shift, axis=0), 0.0)
            shift *= 2
        return slot, rows, 1.0 - f, b

    def operands(slot, rows, k, b):
        q = ps_ref[slot, 0, rows, :]
        qd, kd, q0, decay, us = [], [], [], [], []
        for ci in range(nck):
            sl = slice(ci * c, (ci + 1) * c)
            bc = b[sl, :]
            b_mid = bc[half - 1:half, :]
            b_last = bc[c - 1:c, :]
            qm = q[sl, :] * jnp.exp2(bc - b_mid)
            qd.append(qm.astype(bf16))
            kd.append((k[sl, :] * jnp.exp2(b_mid - bc)).astype(bf16))
            q0.append((qm * jnp.exp2(b_mid)).astype(bf16))
            decay.append(jnp.exp2(b_last))
            us.append(jnp.exp2(b_last - b_mid))
        return slot, rows, qd, kd, q0, decay, us, ps_ref[slot, 2, rows, :].astype(bf16)

    def intra_chunk(slot, rows, qd, kd, q0, decay, us, v):
        qd_all = jnp.concatenate(qd, axis=0)
        kd_all = jnp.concatenate(kd, axis=0)
        scores = []
        for h in range(2):
            ln = slice(h * HEAD, (h + 1) * HEAD)
            s = lax.dot_general(qd_all[:, ln], kd_all[:, ln], nt, preferred_element_type=f32)
            scores.append(jnp.where(causal, s, 0.0).astype(bf16))
        v_diag = jnp.concatenate([jnp.where(head_lane == 0, v, jnp.zeros_like(v)),
                                  jnp.where(head_lane == 1, v, jnp.zeros_like(v))], axis=0)
        o_intra = jnp.dot(jnp.concatenate(scores, axis=1), v_diag, preferred_element_type=f32)
        upd = [[lax.dot_general(v[ci * c:(ci + 1) * c, h * HEAD:(h + 1) * HEAD],
                                kd[ci][:, h * HEAD:(h + 1) * HEAD], tn, preferred_element_type=f32)
                * us[ci][:, h * HEAD:(h + 1) * HEAD] for h in range(2)] for ci in range(nck)]
        return slot, rows, o_intra, upd, q0, decay

    def recur(out_row0, st, slot, rows, o_intra, upd, q0, decay):
        gate = ps_ref[slot, 3, rows, :]
        for ci in range(nck):
            sl = slice(ci * c, (ci + 1) * c)
            s0 = st[0].astype(bf16)
            s1 = st[1].astype(bf16)
            both = jnp.concatenate([jnp.concatenate([s0, zero_blk], axis=1),
                                    jnp.concatenate([zero_blk, s1], axis=1)], axis=0)
            o = o_intra[sl, :] + lax.dot_general(q0[ci], both, nt, preferred_element_type=f32)
            for h in range(2):
                ln = slice(h * HEAD, (h + 1) * HEAD)
                st[h] = st[h] * decay[ci][:, ln] + upd[ci][h]
                oh = o[:, ln]
                oh = oh * lax.rsqrt(jnp.mean(oh * oh, axis=-1, keepdims=True) + EPS)
                gh = gate[sl, ln]
                r0 = out_row0 + rows.start + ci * c
                o_ref[r0:r0 + c, ln] = (oh * ng[:, ln] * (gh * _sigmoid(gh))).astype(o_ref.dtype)
        return st

    def mix(slot, out_row0, st):
        stage = [decays(slot, g) for g in range(HG_BLOCK // rg)]
        stage = [operands(*x) for x in stage]
        stage = [intra_chunk(*x) for x in stage]
        for x in stage:
            st = recur(out_row0, st, *x)
        return st

    st = [st_ref[0], st_ref[1]]
    n_blocks = HG_TILE // HG_BLOCK
    for i in range(n_blocks):
        if i + 1 < n_blocks:
            project(u_ref, (i + 1) * HG_BLOCK, (i + 1) % 2)
        else:
            project(un_ref, 0, 0)
        st = mix(i % 2, i * HG_BLOCK, st)
    st_ref[0] = st[0]
    st_ref[1] = st[1]


def _hgrn(u, w_in, lb, ng, batch, seq):
    n, dm = u.shape
    w = w_in.shape[1] // 4
    pairs = w // (2 * HEAD)
    nt = seq // HG_TILE
    blk = (HG_TILE, 2 * HEAD)

    def wspec(k):
        return pl.BlockSpec((dm, 2 * HEAD), lambda b, pr, t, k=k: (0, k * pairs + pr), pipeline_mode=pl.Buffered(1))

    vec = pl.BlockSpec((1, 2 * HEAD), lambda b, pr, t: (0, pr))
    return pl.pallas_call(
        _hgrn_kernel,
        out_shape=jax.ShapeDtypeStruct((n, w), bf16),
        grid=(batch, pairs, nt),
        in_specs=[pl.BlockSpec((HG_TILE, dm), lambda b, pr, t: (b * nt + t, 0)),
                  pl.BlockSpec((HG_TILE, dm), lambda b, pr, t: (b * nt + jnp.minimum(t + 1, nt - 1), 0)),
                  wspec(0), wspec(1), wspec(2), wspec(3), vec, vec],
        out_specs=pl.BlockSpec(blk, lambda b, pr, t: (b * nt + t, pr)),
        scratch_shapes=[pltpu.VMEM((2, HEAD, HEAD), f32),
                        pltpu.VMEM((2, 4, HG_BLOCK, 2 * HEAD), f32),
                        pltpu.VMEM((4, dm, 2 * HEAD), bf16)],
        compiler_params=_cparams(("parallel", "parallel", "arbitrary")),
        name="hgrn2",
    )(u, u, w_in, w_in, w_in, w_in, lb.reshape(1, w), ng.reshape(1, w))


MERGE_D = 4
LOG2E = math.log2(math.e)
QK_SCALE = HEAD ** -0.5 * LOG2E


def _attn_kernel(u_ref, wq_ref, wk_ref, wv_ref, wg_ref, q1_ref, k1_ref, v1_ref, q2_ref, k2_ref, v2_ref,
                 bias_ref, o_ref, ck0, cv0, ck1, cv1, ck2, cv2, oacc, lacc, macc, nat, p0, w0):
    t = pl.program_id(2)
    par = t % 2
    q_refs = (p0.at[0], q1_ref, q2_ref)
    k_refs = (p0.at[1], k1_ref, k2_ref)
    v_refs = (p0.at[2], v1_ref, v2_ref)
    gate_ref = p0.at[3]
    cks = (ck0, ck1, ck2)
    cvs = (cv0, cv1, cv2)
    blk = ATT_BLOCK
    res_rows = ATT_TILE // MERGE_D

    @pl.when(t == 0)
    def _():
        for g, (_, d) in enumerate(DILATED_GROUPS):
            cks[g][:, 2 * blk:, :] = jnp.zeros((d, blk, HEAD), bf16)
            cvs[g][:, 2 * blk:, :] = jnp.zeros((d, blk, HEAD), bf16)
        for k, w_ref in enumerate((wq_ref, wk_ref, wv_ref, wg_ref)):
            w0[:, k * HEAD:(k + 1) * HEAD] = (w_ref[...] * (QK_SCALE if k == 0 else 1.0)).astype(bf16)

    proj_rows = ROW_CHUNK // 2

    def project(c0):
        y = jnp.dot(u_ref[c0:c0 + proj_rows, :], w0[...], preferred_element_type=f32).astype(bf16)
        for k in range(4):
            p0[k, c0:c0 + proj_rows, :] = y[:, k * HEAD:(k + 1) * HEAD]

    slot = pl.ds(pl.multiple_of(par * 2 * blk, blk), blk)

    def stage_carry(g):
        d = DILATED_GROUPS[g][1]
        per_res = ATT_TILE // d
        for r in range(d):
            for src, dst in ((k_refs[g], cks[g]), (v_refs[g], cvs[g])):
                dst[r, blk:2 * blk, :] = src[r * per_res:r * per_res + blk, :]
                dst[r, slot, :] = src[(r + 1) * per_res - blk:(r + 1) * per_res, :]

    cwin = pl.ds(pl.multiple_of((1 - par) * blk, blk), 2 * blk)
    colid = lax.broadcasted_iota(jnp.int32, (blk, 2 * blk), 1)
    pen = jnp.where(colid >= blk, jnp.where(t == 0, NEG, 0.0).astype(f32), 0.0)

    def carry_bias(g):
        b = bias_ref[g]
        swapped = jnp.concatenate([b[:, blk:], b[:, :blk]], axis=1)
        return jnp.where(par == 1, b, swapped) + pen

    carry_biases = [carry_bias(g) for g in range(N_GROUPS)]

    def attend(q, kwin, vwin, bias):
        s = lax.dot_general(q, kwin, (((1,), (1,)), ((), ())), preferred_element_type=f32)
        s = s + bias
        m = jnp.max(s, axis=-1, keepdims=True)
        p = jnp.exp2(s - m)
        l = jnp.sum(p, axis=-1, keepdims=True)
        acc = jnp.dot(p.astype(bf16), vwin, preferred_element_type=f32)
        return acc, jnp.broadcast_to(l, (blk, HEAD)), jnp.broadcast_to(m, (blk, HEAD))

    def block(g, r, j):
        per_res = ATT_TILE // DILATED_GROUPS[g][1]
        q = q_refs[g][r * per_res + j * blk:r * per_res + (j + 1) * blk, :]
        if j == 0:
            return attend(q, cks[g][r, cwin, :], cvs[g][r, cwin, :], carry_biases[g])
        win = slice(r * per_res + (j - 1) * blk, r * per_res + (j + 1) * blk)
        return attend(q, k_refs[g][win, :], v_refs[g][win, :], bias_ref[g])

    dsts = (oacc, lacc, macc)

    def group1(r, j):
        row0 = r * (ATT_TILE // d1) + j * blk
        for k, val in enumerate(block(1, r, j)):
            dsts[k][1, row0:row0 + blk, :] = val

    def group2(r16):
        rows = pl.ds((r16 % MERGE_D) * res_rows + r16 // MERGE_D, blk, stride=d2 // MERGE_D)
        for k, val in enumerate(block(2, r16, 0)):
            dsts[k][2, rows, :] = val

    def group0(j):
        for k, val in enumerate(block(0, 0, j)):
            dsts[k][0, j * blk:(j + 1) * blk, :] = val

    d1 = DILATED_GROUPS[1][1]
    d2 = DILATED_GROUPS[2][1]
    stage_carry(1)
    stage_carry(2)
    dilated = ([functools.partial(group1, r, j) for r in range(d1) for j in range(ATT_TILE // d1 // blk)]
               + [functools.partial(group2, r16) for r16 in range(d2)])
    n_proj = ATT_TILE // proj_rows
    per_proj = len(dilated) // n_proj
    for i in range(n_proj):
        project(i * proj_rows)
        for task in dilated[i * per_proj:(i + 1) * per_proj]:
            task()
    stage_carry(0)
    for j in range(ATT_TILE // blk):
        group0(j)

    def merge(c, carry):
        for r4 in range(MERGE_D):
            rows = pl.ds(pl.multiple_of(r4 * res_rows + c * blk, blk), blk)
            toks = pl.ds(c * blk * MERGE_D + r4, blk, stride=MERGE_D)
            at = (toks, rows, rows)
            ms = [macc[g, at[g], :] for g in range(N_GROUPS)]
            mx = functools.reduce(jnp.maximum, ms)
            ws = [jnp.exp2(x - mx) for x in ms]
            num = functools.reduce(lambda a, b: a + b, [w * oacc[g, at[g], :] for g, w in enumerate(ws)])
            den = functools.reduce(lambda a, b: a + b, [w * lacc[g, at[g], :] for g, w in enumerate(ws)])
            nat[toks, :] = num / den
        return carry

    lax.fori_loop(0, res_rows // blk, merge, 0)

    rc = 256
    for c0 in range(0, ATT_TILE, rc):
        rows = slice(c0, c0 + rc)
        gate = gate_ref[rows, :].astype(f32)
        o_ref[rows, :] = (nat[rows, :] * (gate * _sigmoid(gate))).astype(o_ref.dtype)


def _attention(u, w_in, p, bias, batch, seq, heads):
    n, dm = u.shape
    nt = seq // ATT_TILE
    blk = (ATT_TILE, HEAD)

    def spec(k):
        return pl.BlockSpec(blk, lambda b, h, t, k=k: (b * nt + t, k * heads + h))

    def wspec(k):
        return pl.BlockSpec((dm, HEAD), lambda b, h, t, k=k: (0, k * heads + h))

    scratch = []
    for _, d in DILATED_GROUPS:
        scratch += [pltpu.VMEM((d, 3 * ATT_BLOCK, HEAD), bf16)] * 2
    scratch += [pltpu.VMEM((N_GROUPS, ATT_TILE, HEAD), f32)] * 3
    scratch += [pltpu.VMEM((ATT_TILE, HEAD), f32)]
    scratch += [pltpu.VMEM((4, ATT_TILE, HEAD), bf16), pltpu.VMEM((dm, 4 * HEAD), bf16)]
    return pl.pallas_call(
        _attn_kernel,
        out_shape=jax.ShapeDtypeStruct((n, heads * HEAD), bf16),
        grid=(batch, heads, nt),
        in_specs=[pl.BlockSpec((ATT_TILE, dm), lambda b, h, t: (b * nt + t, 0)),
                  wspec(0), wspec(1), wspec(2), wspec(3 * N_GROUPS)] + [spec(k) for k in range(6)] + [
                  pl.BlockSpec((N_GROUPS, None, ATT_BLOCK, 2 * ATT_BLOCK), lambda b, h, t: (0, h, 0, 0))],
        out_specs=pl.BlockSpec(blk, lambda b, h, t: (b * nt + t, h)),
        scratch_shapes=scratch,
        compiler_params=_cparams(("parallel", "parallel", "arbitrary")),
        name="dilated_attention",
    )(u, w_in, w_in, w_in, w_in, *([p] * 6), bias)


def _t5_bucket(dist):
    max_exact = N_BUCKETS // 2
    df = jnp.maximum(dist, 1).astype(f32)
    large = max_exact + (jnp.log(df / max_exact) / math.log(MAX_DISTANCE / max_exact)
                         * (N_BUCKETS - max_exact)).astype(jnp.int32)
    large = jnp.minimum(large, N_BUCKETS - 1)
    return jnp.where(dist < max_exact, dist, large)


def _bias_tables(rel_bias, heads):
    a = jnp.arange(ATT_BLOCK)[:, None]
    c = jnp.arange(2 * ATT_BLOCK)[None, :]
    rel = ATT_BLOCK + a - c
    tables = []
    for g, (window, d) in enumerate(DILATED_GROUPS):
        assert window // d == ATT_BLOCK and ATT_TILE % (d * ATT_BLOCK) == 0
        band = (rel >= 0) & (rel <= window // d)
        onehot = jax.nn.one_hot(_t5_bucket(jnp.maximum(rel, 0) * d), N_BUCKETS, dtype=f32)
        tab = jnp.einsum("acb,bh->hac", onehot, rel_bias[:, g * heads:(g + 1) * heads].astype(f32),
                         precision=lax.Precision.HIGHEST)
        tables.append(jnp.where(band[None], tab * LOG2E, NEG))
    return jnp.stack(tables, axis=0)


PERM_ROWS = 256


def _groupproj_kernel(a_ref, w_ref, o_ref, perm_ref, *, col_blocks_per_group):
    j = pl.program_id(1)
    tm = a_ref.shape[0]

    @pl.when(j == 0)
    def _():
        i = lax.broadcasted_iota(jnp.int32, (PERM_ROWS, PERM_ROWS), 0)
        k = lax.broadcasted_iota(jnp.int32, (PERM_ROWS, PERM_ROWS), 1)
        for g, (_, d) in enumerate(DILATED_GROUPS[1:]):
            per = PERM_ROWS // d
            perm = jnp.where(k == (i % per) * d + i // per, 1.0, 0.0).astype(bf16)
            for sb in range(tm // PERM_ROWS):
                y = jnp.dot(perm, a_ref[sb * PERM_ROWS:(sb + 1) * PERM_ROWS, :],
                            preferred_element_type=f32).astype(bf16)
                for r in range(d):
                    dst = r * (tm // d) + sb * per
                    perm_ref[g, dst:dst + per, :] = y[r * per:(r + 1) * per, :]

    def project(lhs_ref):
        is_q = (j % col_blocks_per_group) < col_blocks_per_group // 3
        w = (w_ref[...] * jnp.where(is_q, QK_SCALE, 1.0)).astype(bf16)
        rc = ROW_CHUNK
        for r in range(tm // rc):
            rows = slice(r * rc, (r + 1) * rc)
            o_ref[rows, :] = jnp.dot(lhs_ref[rows, :], w, preferred_element_type=f32).astype(o_ref.dtype)

    grp = j // col_blocks_per_group
    for g in range(N_GROUPS - 1):
        pl.when(grp == g)(functools.partial(project, perm_ref.at[g]))


def _groupproj(u, w, width, tn=2048):
    n, k = u.shape
    c = 3 * (N_GROUPS - 1) * width
    tn = math.gcd(tn, width)
    first = 3 * width // tn
    return pl.pallas_call(
        functools.partial(_groupproj_kernel, col_blocks_per_group=3 * width // tn),
        out_shape=jax.ShapeDtypeStruct((n, c), bf16),
        grid=(n // ATT_TILE, c // tn),
        in_specs=[pl.BlockSpec((ATT_TILE, k), lambda i, j: (i, 0)),
                  pl.BlockSpec((k, tn), lambda i, j: (0, j + first))],
        out_specs=pl.BlockSpec((ATT_TILE, tn), lambda i, j: (i, j)),
        scratch_shapes=[pltpu.VMEM((N_GROUPS - 1, ATT_TILE, k), bf16)],
        compiler_params=_cparams(("parallel", "arbitrary")),
        name="group_proj",
    )(u, w)


def kernel(x, ln_g, hg_w_in, hg_lb_logits, hg_norm_g, hg_w_out, att_w_in, att_w_out, rel_bias, final_g):
    batch, seq, d_model = x.shape
    n = batch * seq
    w = hg_w_out.shape[1]
    heads = w // HEAD
    assert seq % ATT_TILE == 0 and seq % HG_TILE == 0 and w % (2 * HEAD) == 0

    lower = jnp.cumsum(jax.nn.softmax(hg_lb_logits.astype(f32), axis=0), axis=0)
    h0 = x.reshape(n, d_model)

    u0 = _rmsnorm(h0, ln_g[0], bf16)
    y0 = _hgrn(u0, hg_w_in[0], lower[0], hg_norm_g[0], batch, seq)
    h1, u1 = _outproj(y0, hg_w_out[0], h0, ln_g[1], bf16, emit_h=True)

    wa = att_w_in[0]
    y1 = _attention(u1, wa, _groupproj(u1, wa, w), _bias_tables(rel_bias, heads), batch, seq, heads)
    (out,) = _outproj(y1, att_w_out[0], h1, final_g, f32, emit_h=False)
    return out.reshape(batch, seq, d_model)
```
